```python
import functools
import jax, jax.numpy as jnp
from jax import lax
import numpy as np

D_MODEL = 1024
BATCH = 8
SEQ = 2048
DEPTH = 4
DEC_BATCH = 128
DEC_SEQ = 1
PAST_LEN = 2048
PAGE_SIZE = 128

N_HEADS = 16
HEAD_DIM = 64
D_ATTN = N_HEADS * HEAD_DIM
D_CONV = D_MODEL
CONV_WIDTH = 3
D_FF = -(-8 * D_MODEL // (3 * 256)) * 256
D_PLE = 256
Q_BLOCK = 128
RMS_EPS = 1e-6
IN_WIDTHS = [D_CONV, D_CONV, D_CONV, D_ATTN, D_ATTN, D_ATTN, N_HEADS, D_MODEL, D_MODEL]
D_IN = sum(IN_WIDTHS)
SPLIT_POINTS = [int(s) for s in np.cumsum(IN_WIDTHS)[:-1]]

kernel_name = 'fox_shortconv_hybrid_step'


def rmsnorm(x, g):
    xf = x.astype(jnp.float32)
    y = xf * lax.rsqrt(jnp.mean(xf * xf, axis=-1, keepdims=True) + RMS_EPS)
    return (y * g.astype(jnp.float32)).astype(x.dtype)


def fox_block(q, k, v, cq, ck, qpos, kpos):
    s = jnp.einsum('bqhd,bkhd->bhqk', q, k).astype(jnp.float32) * (HEAD_DIM ** -0.5)
    bias = jnp.swapaxes(cq, 1, 2)[:, :, :, None] - jnp.swapaxes(ck, 1, 2)[:, :, None, :]
    mask = kpos[None, :] <= qpos[:, None]
    s = jnp.where(mask, s + bias, -jnp.inf)
    p = jax.nn.softmax(s, axis=-1).astype(v.dtype)
    return jnp.einsum('bhqk,bkhd->bqhd', p, v)


def prompt_attend(q, k, v, logf):
    b, L = q.shape[0], q.shape[1]
    nb = L // Q_BLOCK
    cum = jnp.cumsum(logf.astype(jnp.float32), axis=1)
    pos = jnp.arange(L)
    qb = q.reshape(b, nb, Q_BLOCK, N_HEADS, HEAD_DIM).swapaxes(0, 1)
    cqb = cum.reshape(b, nb, Q_BLOCK, N_HEADS).swapaxes(0, 1)
    pb = pos.reshape(nb, Q_BLOCK)
    o = lax.map(lambda a: fox_block(a[0], k, v, a[1], cum, a[2], pos), (qb, cqb, pb))
    return o.swapaxes(0, 1).reshape(b, L, N_HEADS, HEAD_DIM)


def sample_attend(q, k, v, logf, past_k, past_v, past_logf):
    P, L = past_k.shape[1], q.shape[1]
    kk = jnp.concatenate([past_k.astype(k.dtype), k], axis=1)
    vv = jnp.concatenate([past_v.astype(v.dtype), v], axis=1)
    cum = jnp.cumsum(jnp.concatenate([past_logf.astype(jnp.float32), logf.astype(jnp.float32)], axis=1), axis=1)
    kpos = jnp.arange(P + L)
    qpos = P + jnp.arange(L)
    return fox_block(q, kk, vv, cum[:, P:], cum, qpos, kpos)


def layer(h, p_i, conv_prev, attend, w_in, b_forget, conv_w, w_conv_out, w_attn_out, w_o,
          g_mix, g_ffn, w_gate, w_up, w_down, g_ple, w_ple_gate, w_ple_proj):
    b, L = h.shape[0], h.shape[1]
    hn = rmsnorm(h, g_mix)
    z = hn @ w_in
    bg, cg, xin, q, k, v, fl, gc, ga = jnp.split(z, SPLIT_POINTS, axis=-1)
    q = q.reshape(b, L, N_HEADS, HEAD_DIM)
    k = k.reshape(b, L, N_HEADS, HEAD_DIM)
    v = v.reshape(b, L, N_HEADS, HEAD_DIM)
    logf = jax.nn.log_sigmoid(fl.astype(jnp.float32) + b_forget.astype(jnp.float32))
    u = cg * xin
    uc = jnp.concatenate([conv_prev.astype(u.dtype), u], axis=1)
    conv = conv_w[0] * uc[:, 0:L]
    for j in range(1, CONV_WIDTH):
        conv = conv + conv_w[j] * uc[:, j:j + L]
    y_conv = bg * conv
    o = attend(q, k, v, logf).reshape(b, L, D_ATTN)
    merged = jax.nn.sigmoid(gc) * (y_conv @ w_conv_out) + jax.nn.sigmoid(ga) * (o @ w_attn_out)
    h = h + merged @ w_o
    hn = rmsnorm(h, g_ffn)
    h = h + (jax.nn.silu(hn @ w_gate) * (hn @ w_up)) @ w_down
    hn = rmsnorm(h, g_ple)
    h = h + jax.nn.sigmoid(hn @ w_ple_gate) * (p_i @ w_ple_proj)
    return h, k, v, logf, uc[:, uc.shape[1] - (CONV_WIDTH - 1):]


def setup_inputs(seed: int = 0) -> dict:
    key = jax.random.key(seed)
    ks = jax.random.split(key, 32)
    n_pages = PAST_LEN // PAGE_SIZE
    n_phys = (5 * DEC_BATCH * n_pages) // 4
    f32 = jnp.float32
    nrm = lambda k, shape, scale: jax.random.normal(k, shape, f32) * scale
    gain = lambda k, shape: 1.0 + 0.05 * jax.random.normal(k, shape, f32)
    page_table = jax.random.permutation(ks[0], n_phys)[:DEC_BATCH * n_pages].reshape(DEC_BATCH, n_pages).astype(jnp.int32)
    return {
        'x_prompt': nrm(ks[1], (BATCH, SEQ, D_MODEL), 1.0),
        'x_sample': nrm(ks[2], (DEC_BATCH, DEC_SEQ, D_MODEL), 1.0),
        'cache_k': nrm(ks[3], (DEPTH, n_phys, PAGE_SIZE, N_HEADS, HEAD_DIM), 1.0),
        'cache_v': nrm(ks[4], (DEPTH, n_phys, PAGE_SIZE, N_HEADS, HEAD_DIM), 1.0),
        'cache_logf': jax.nn.log_sigmoid(3.0 + jax.random.normal(ks[5], (DEPTH, n_phys, PAGE_SIZE, N_HEADS), f32)),
        'state_conv': nrm(ks[6], (DEPTH, DEC_BATCH, CONV_WIDTH - 1, D_CONV), 1.0),
        'page_table': page_table,
        'p_prompt': nrm(ks[7], (DEPTH, BATCH, SEQ, D_PLE), 1.0),
        'p_sample': nrm(ks[8], (DEPTH, DEC_BATCH, DEC_SEQ, D_PLE), 1.0),
        'w_in': nrm(ks[9], (DEPTH, D_MODEL, D_IN), D_MODEL ** -0.5),
        'b_forget': 3.0 + 0.5 * jax.random.normal(ks[10], (DEPTH, N_HEADS), f32),
        'conv_w': nrm(ks[11], (DEPTH, CONV_WIDTH, D_CONV), CONV_WIDTH ** -0.5),
        'w_conv_out': nrm(ks[12], (DEPTH, D_CONV, D_MODEL), D_CONV ** -0.5),
        'w_attn_out': nrm(ks[13], (DEPTH, D_ATTN, D_MODEL), D_ATTN ** -0.5),
        'w_o': nrm(ks[14], (DEPTH, D_MODEL, D_MODEL), D_MODEL ** -0.5),
        'g_mix': gain(ks[15], (DEPTH, D_MODEL)),
        'g_ffn': gain(ks[16], (DEPTH, D_MODEL)),
        'w_gate': nrm(ks[17], (DEPTH, D_MODEL, D_FF), D_MODEL ** -0.5),
        'w_up': nrm(ks[18], (DEPTH, D_MODEL, D_FF), D_MODEL ** -0.5),
        'w_down': nrm(ks[19], (DEPTH, D_FF, D_MODEL), D_FF ** -0.5),
        'g_ple': gain(ks[20], (DEPTH, D_MODEL)),
        'w_ple_gate': nrm(ks[21], (DEPTH, D_MODEL, D_MODEL), D_MODEL ** -0.5),
        'w_ple_proj': nrm(ks[22], (DEPTH, D_PLE, D_MODEL), D_PLE ** -0.5),
        'g_final': gain(ks[23], (D_MODEL,)),
    }


def reference(x_prompt, x_sample, cache_k, cache_v, cache_logf, state_conv, page_table, p_prompt, p_sample,
              w_in, b_forget, conv_w, w_conv_out, w_attn_out, w_o, g_mix, g_ffn, w_gate, w_up, w_down,
              g_ple, w_ple_gate, w_ple_proj, g_final):
    db = x_sample.shape[0]
    past = page_table.shape[1] * PAGE_SIZE
    hp, hs = x_prompt, x_sample
    conv0 = jnp.zeros((x_prompt.shape[0], CONV_WIDTH - 1, D_CONV), x_prompt.dtype)
    kp_l, vp_l, lp_l, cp_l, ks_l, vs_l, ls_l, cs_l = [], [], [], [], [], [], [], []
    for i in range(DEPTH):
        lw = (w_in[i], b_forget[i], conv_w[i], w_conv_out[i], w_attn_out[i], w_o[i], g_mix[i], g_ffn[i],
              w_gate[i], w_up[i], w_down[i], g_ple[i], w_ple_gate[i], w_ple_proj[i])
        hp, kp, vp, lp, cp = layer(hp, p_prompt[i], conv0, prompt_attend, *lw)
        past_k = cache_k[i][page_table].reshape(db, past, N_HEADS, HEAD_DIM)
        past_v = cache_v[i][page_table].reshape(db, past, N_HEADS, HEAD_DIM)
        past_logf = cache_logf[i][page_table].reshape(db, past, N_HEADS)
        attend_s = functools.partial(sample_attend, past_k=past_k, past_v=past_v, past_logf=past_logf)
        hs, ksm, vsm, lsm, csm = layer(hs, p_sample[i], state_conv[i], attend_s, *lw)
        kp_l.append(kp); vp_l.append(vp); lp_l.append(lp); cp_l.append(cp)
        ks_l.append(ksm); vs_l.append(vsm); ls_l.append(lsm); cs_l.append(csm)
    y_prompt = rmsnorm(hp, g_final)
    y_sample = rmsnorm(hs, g_final)
    return (y_prompt, y_sample,
            jnp.stack(kp_l), jnp.stack(vp_l), jnp.stack(lp_l), jnp.stack(cp_l),
            jnp.stack(ks_l), jnp.stack(vs_l), jnp.stack(ls_l), jnp.stack(cs_l))
```

```python
import functools

import jax
import jax.numpy as jnp
from jax import lax
from jax.experimental import pallas as pl
from jax.experimental.pallas import tpu as pltpu

N_HEADS = 16
HEAD_DIM = 64
CONV_WIDTH = 3
RMS_EPS = 1e-6
LANES = 128
SUBLANES = 8
N_PIECES = 3
VMEM_LIMIT = 56 * 1024 * 1024

F32 = jnp.float32
BF16 = jnp.bfloat16


def _dot(a, b):
    return jnp.dot(a, b, preferred_element_type=F32)


def _dot_nt(a, b):
    return lax.dot_general(a, b, (((1,), (1,)), ((), ())), preferred_element_type=F32)


def _rms(x, g):
    return x * lax.rsqrt(jnp.mean(x * x, axis=-1, keepdims=True) + RMS_EPS) * g


def _sigmoid(x):
    return 1.0 / (1.0 + jnp.exp(-x))


def _log_sigmoid(x):
    return jnp.minimum(x, 0.0) - jnp.log1p(jnp.exp(-jnp.abs(x)))


def _split3(x):
    p1 = x.astype(BF16).astype(F32)
    r = x - p1
    p2 = r.astype(BF16).astype(F32)
    p3 = (r - p2).astype(BF16).astype(F32)
    return p1, p2, p3


def _resident(shape):
    nd = len(shape)
    return pl.BlockSpec(shape, lambda *_: (0,) * nd, pipeline_mode=pl.Buffered(1))


def _inproj_prompt_kernel(h_ref, g_ref, wa_ref, wfl_ref, wg_ref, bf_ref, cw_ref, pq_ref, pk_ref,
                          oq_ref, ok_ref, tri_ref,
                          k32_ref, v32_ref, lf_ref, qa_ref, ka_ref, vb_ref, yc_ref, sgc_ref,
                          sga_ref, cl_ref,
                          ubuf, ccar, *, tm, d):
    nh = N_HEADS

    @pl.when(pl.program_id(1) == 0)
    def _():
        ubuf[0:SUBLANES, :] = jnp.zeros((SUBLANES, d), F32)
        ccar[...] = jnp.zeros_like(ccar)

    hn = _rms(h_ref[...], g_ref[...]).astype(BF16)

    bg = _dot(hn, wa_ref[:, 0:d])
    u = _dot(hn, wa_ref[:, d:2 * d]) * _dot(hn, wa_ref[:, 2 * d:3 * d])
    ubuf[SUBLANES:SUBLANES + tm, :] = u
    um1 = ubuf[SUBLANES - 1:SUBLANES - 1 + tm, :]
    um2 = ubuf[SUBLANES - 2:SUBLANES - 2 + tm, :]
    conv = cw_ref[0:1, :] * um2 + cw_ref[1:2, :] * um1 + cw_ref[2:3, :] * u
    yc_ref[...] = (bg * conv).astype(BF16)
    ubuf[0:SUBLANES, :] = u[tm - SUBLANES:tm, :]
    cl_ref[...] = u[tm - (CONV_WIDTH - 1):tm, :]

    lf3 = _log_sigmoid(_dot(hn, wfl_ref[...]) + bf_ref[...])
    lf_ref[...] = lf3[:, 0:nh]
    l1, l2, l3 = _split3(lf3)
    tri = tri_ref[...]
    cum = (_dot(tri, l1.astype(BF16)) + _dot(tri, l2.astype(BF16)) + _dot(tri, l3.astype(BF16))
           + ccar[...])
    ccar[...] = cum[tm - 1:tm, :]
    c1, c2, c3 = _split3(cum)
    lane3 = lax.broadcasted_iota(jnp.int32, cum.shape, 1)
    cpieces = jnp.where(lane3 < nh, c1, jnp.where(lane3 < 2 * nh, c2, c3)).astype(BF16)
    xq = _dot(cpieces, pq_ref[...]) + oq_ref[...]
    xk = _dot(cpieces, pk_ref[...]) + ok_ref[...]

    q = _dot(hn, wa_ref[:, 3 * d:4 * d]) * (HEAD_DIM ** -0.5)
    k = _dot(hn, wa_ref[:, 4 * d:5 * d])
    k32_ref[...] = k
    low = lax.broadcasted_iota(jnp.int32, (tm, LANES), 1) < HEAD_DIM
    for p in range(nh // 2):
        sl = slice(p * LANES, (p + 1) * LANES)
        ev = slice(2 * p * LANES, (2 * p + 1) * LANES)
        od = slice((2 * p + 1) * LANES, (2 * p + 2) * LANES)
        qa_ref[:, ev] = jnp.where(low, q[:, sl], xq[:, sl]).astype(BF16)
        qa_ref[:, od] = jnp.where(low, xq[:, sl], q[:, sl]).astype(BF16)
        ka_ref[:, ev] = jnp.where(low, k[:, sl], xk[:, sl]).astype(BF16)
        ka_ref[:, od] = jnp.where(low, xk[:, sl], k[:, sl]).astype(BF16)

    v = _dot(hn, wa_ref[:, 5 * d:6 * d])
    v32_ref[...] = v
    vb_ref[...] = v.astype(BF16)
    sgc_ref[...] = _sigmoid(_dot(hn, wg_ref[:, 0:d])).astype(BF16)
    sga_ref[...] = _sigmoid(_dot(hn, wg_ref[:, d:2 * d])).astype(BF16)


def _inproj_sample_kernel(h_ref, g_ref, wa_ref, wfl_ref, wg_ref, bf_ref, cw_ref, um2_ref, um1_ref,
                          k32_ref, v32_ref, lf_ref, q_ref, u_ref, yc_ref, sgc_ref, sga_ref, *, d):
    nh = N_HEADS
    hn = _rms(h_ref[...], g_ref[...]).astype(BF16)
    bg = _dot(hn, wa_ref[:, 0:d])
    u = _dot(hn, wa_ref[:, d:2 * d]) * _dot(hn, wa_ref[:, 2 * d:3 * d])
    conv = cw_ref[0:1, :] * um2_ref[...] + cw_ref[1:2, :] * um1_ref[...] + cw_ref[2:3, :] * u
    yc_ref[...] = (bg * conv).astype(BF16)
    u_ref[...] = u
    lf3 = _log_sigmoid(_dot(hn, wfl_ref[...]) + bf_ref[...])
    lf_ref[...] = lf3[:, 0:nh]
    q_ref[...] = _dot(hn, wa_ref[:, 3 * d:4 * d]) * (HEAD_DIM ** -0.5)
    k32_ref[...] = _dot(hn, wa_ref[:, 4 * d:5 * d])
    v32_ref[...] = _dot(hn, wa_ref[:, 5 * d:6 * d])
    sgc_ref[...] = _sigmoid(_dot(hn, wg_ref[:, 0:d])).astype(BF16)
    sga_ref[...] = _sigmoid(_dot(hn, wg_ref[:, d:2 * d])).astype(BF16)


def _extras_constants(d):
    import numpy as np
    nh = N_HEADS
    pq = np.zeros((N_PIECES * nh, d), np.float32)
    pk = np.zeros((N_PIECES * nh, d), np.float32)
    oq = np.zeros((1, d), np.float32)
    ok = np.zeros((1, d), np.float32)
    for h in range(nh):
        base = (h // 2) * LANES + (HEAD_DIM if h % 2 == 0 else 0)
        for j in range(N_PIECES):
            pq[j * nh + h, base + j] = 1.0
            pk[j * nh + h, base + N_PIECES + j] = -1.0
            oq[0, base + N_PIECES + j] = 1.0
            ok[0, base + j] = 1.0
    return (jnp.asarray(pq, BF16), jnp.asarray(pk, BF16), jnp.asarray(oq), jnp.asarray(ok))


def _inproj_prompt(h, lw, consts, tm):
    b, l, d = h.shape
    nh = N_HEADS
    pq, pk, oq, ok, tri = consts
    grid = (b, l // tm)
    row = lambda n: pl.BlockSpec((None, tm, n), lambda i, j: (i, j, 0))
    in_specs = [row(d), _resident((1, d)), _resident(lw['wa'].shape), _resident(lw['wfl'].shape),
                _resident(lw['wg'].shape), _resident((1, N_PIECES * nh)),
                _resident((CONV_WIDTH, d)), _resident(pq.shape), _resident(pk.shape),
                _resident(oq.shape), _resident(ok.shape), _resident(tri.shape)]
    out_shape = [jax.ShapeDtypeStruct((b, l, d), F32), jax.ShapeDtypeStruct((b, l, d), F32),
                 jax.ShapeDtypeStruct((b, l, nh), F32),
                 jax.ShapeDtypeStruct((b, l, 2 * d), BF16), jax.ShapeDtypeStruct((b, l, 2 * d), BF16),
                 jax.ShapeDtypeStruct((b, l, d), BF16), jax.ShapeDtypeStruct((b, l, d), BF16),
                 jax.ShapeDtypeStruct((b, l, d), BF16), jax.ShapeDtypeStruct((b, l, d), BF16),
                 jax.ShapeDtypeStruct((b, CONV_WIDTH - 1, d), F32)]
    out_specs = [row(d), row(d), row(nh), row(2 * d), row(2 * d), row(d), row(d), row(d), row(d),
                 pl.BlockSpec((None, CONV_WIDTH - 1, d), lambda i, j: (i, 0, 0))]
    return pl.pallas_call(
        functools.partial(_inproj_prompt_kernel, tm=tm, d=d),
        grid=grid, in_specs=in_specs, out_specs=out_specs, out_shape=out_shape,
        scratch_shapes=[pltpu.VMEM((tm + SUBLANES, d), F32), pltpu.VMEM((1, N_PIECES * nh), F32)],
        compiler_params=pltpu.CompilerParams(dimension_semantics=("arbitrary", "arbitrary"),
                                             vmem_limit_bytes=VMEM_LIMIT),
        name="inproj_prompt",
    )(h, lw['g_mix'], lw['wa'], lw['wfl'], lw['wg'], lw['bf3'], lw['conv_w'], pq, pk, oq, ok, tri)


def _inproj_sample(h, um2, um1, lw):
    m, d = h.shape
    nh = N_HEADS
    full = lambda n, dt: jax.ShapeDtypeStruct((m, n), dt)
    out_shape = [full(d, F32), full(d, F32), full(nh, F32), full(d, F32), full(d, F32),
                 full(d, BF16), full(d, BF16), full(d, BF16)]
    return pl.pallas_call(
        functools.partial(_inproj_sample_kernel, d=d),
        out_shape=out_shape,
        compiler_params=pltpu.CompilerParams(vmem_limit_bytes=VMEM_LIMIT),
        name="inproj_sample",
    )(h, lw['g_mix'], lw['wa'], lw['wfl'], lw['wg'], lw['bf3'], lw['conv_w'], um2, um1)


def _attn_prompt_kernel(qa_ref, ka_ref, v_ref, o_ref, *, tq):
    qi = pl.program_id(2)
    low = lax.broadcasted_iota(jnp.int32, (tq, LANES), 1) < HEAD_DIM
    rows = lax.broadcasted_iota(jnp.int32, (tq, tq), 0)
    cols = lax.broadcasted_iota(jnp.int32, (tq, tq), 1)
    causal = cols <= rows
    halves = []
    for par in range(2):
        hs = slice(par * LANES, (par + 1) * LANES)
        qh = qa_ref[:, hs]

        def block(j, carry, masked):
            m, l, acc = carry
            start = pl.multiple_of(j * tq, tq)
            s = _dot_nt(qh, ka_ref[pl.ds(start, tq), hs])
            if masked:
                s = jnp.where(causal, s, -jnp.inf)
            m_new = jnp.maximum(m, jnp.max(s, axis=1, keepdims=True))
            alpha = jnp.exp(m - m_new)
            p = jnp.exp(s - m_new)
            l = alpha * l + jnp.sum(p, axis=1, keepdims=True)
            acc = alpha * acc + _dot(p.astype(BF16), v_ref[pl.ds(start, tq), :])
            return m_new, l, acc

        init = (jnp.full((tq, 1), -jnp.inf, F32), jnp.zeros((tq, 1), F32),
                jnp.zeros((tq, LANES), F32))
        carry = lax.fori_loop(0, qi, functools.partial(block, masked=False), init)
        m, l, acc = block(qi, carry, True)
        halves.append(acc / l)
    o_ref[...] = jnp.where(low, halves[0], halves[1]).astype(o_ref.dtype)


def _attn_prompt(qa, ka, vb, tq):
    b, l, d2 = qa.shape
    d = d2 // 2
    npair = N_HEADS // 2
    grid = (b, npair, l // tq)
    return pl.pallas_call(
        functools.partial(_attn_prompt_kernel, tq=tq),
        grid=grid,
        in_specs=[pl.BlockSpec((None, tq, 2 * LANES), lambda i, p, j: (i, j, p)),
                  pl.BlockSpec((None, l, 2 * LANES), lambda i, p, j: (i, 0, p)),
                  pl.BlockSpec((None, l, LANES), lambda i, p, j: (i, 0, p))],
        out_specs=pl.BlockSpec((None, tq, LANES), lambda i, p, j: (i, j, p)),
        out_shape=jax.ShapeDtypeStruct((b, l, d), BF16),
        compiler_params=pltpu.CompilerParams(
            dimension_semantics=("arbitrary", "arbitrary", "arbitrary"),
            vmem_limit_bytes=VMEM_LIMIT),
        name="attn_prompt",
    )(qa, ka, vb)


def _attn_sample_kernel(pt_ref, q_ref, kn_ref, vn_ref, ln_ref, eye_ref, at_ref, *refs, pg, page):
    del pt_ref
    k_refs = refs[0:pg]
    v_refs = refs[pg:2 * pg]
    lf_refs = refs[2 * pg:3 * pg]
    o_ref = refs[3 * pg]
    m_sc, l_sc, acc_sc, car_sc = refs[3 * pg + 1:]
    nh = N_HEADS
    rows = page * nh
    g = pl.program_id(1)

    q = q_ref[...]

    @pl.when(g == 0)
    def _():
        m_sc[...] = jnp.sum(q * kn_ref[...], axis=1, keepdims=True)
        l_sc[...] = jnp.ones_like(l_sc)
        acc_sc[...] = vn_ref[...]
        car_sc[...] = ln_ref[...]

    qb = q.astype(BF16)
    eye = eye_ref[...]
    at = at_ref[...]
    own = (lax.broadcasted_iota(jnp.int32, (nh, rows), 1) % nh
           == lax.broadcasted_iota(jnp.int32, (nh, rows), 0))
    m, l, acc, car = m_sc[...], l_sc[...], acc_sc[...], car_sc[...]
    for j in range(pg):
        kx = k_refs[j][...].reshape(rows, HEAD_DIM).astype(BF16)
        vx = v_refs[j][...].reshape(rows, HEAD_DIM).astype(BF16)
        st = _dot_nt(qb, kx)
        pieces = [_dot_nt(eye, lp.astype(BF16)).astype(BF16) for lp in _split3(lf_refs[j][...])]
        sfx = _dot(jnp.concatenate(pieces, axis=0), at)
        sfx = sfx[0:nh] + sfx[nh:2 * nh] + sfx[2 * nh:3 * nh]
        z = jnp.where(own, st + sfx[:, 0:rows] + car, -jnp.inf)
        m_new = jnp.maximum(m, jnp.max(z, axis=1, keepdims=True))
        alpha = jnp.exp(m - m_new)
        p = jnp.exp(z - m_new)
        l = alpha * l + jnp.sum(p, axis=1, keepdims=True)
        acc = alpha * acc + _dot(p.astype(BF16), vx)
        m = m_new
        car = car + sfx[:, rows:rows + 1]
    m_sc[...], l_sc[...], acc_sc[...], car_sc[...] = m, l, acc, car

    @pl.when(g == pl.num_programs(1) - 1)
    def _():
        o_ref[...] = acc / l


def _attn_sample(layer, page_table, q, kn, vn, ln, cache_k, cache_v, cache_logf, eye, at, pg):
    nb, nh, hd = q.shape
    page = cache_k.shape[2]
    n_pages = page_table.shape[1]
    groups = n_pages // pg
    pt = page_table.reshape(-1)

    def page_map(j):
        return lambda b, g, pt_ref: (layer, pt_ref[b * n_pages + n_pages - 1 - (g * pg + j)], 0, 0, 0)

    def lf_map(j):
        return lambda b, g, pt_ref: (layer, pt_ref[b * n_pages + n_pages - 1 - (g * pg + j)], 0, 0)

    seq3 = lambda n: pl.BlockSpec((None, nh, n), lambda b, g, pt_ref: (b, 0, 0))
    const = lambda a: pl.BlockSpec(a.shape, lambda b, g, pt_ref: (0,) * a.ndim)
    in_specs = ([seq3(hd), seq3(hd), seq3(hd), seq3(1), const(eye), const(at)]
                + [pl.BlockSpec((None, None, page, nh, hd), page_map(j)) for j in range(pg)]
                + [pl.BlockSpec((None, None, page, nh, hd), page_map(j)) for j in range(pg)]
                + [pl.BlockSpec((None, None, page, nh), lf_map(j)) for j in range(pg)])
    grid_spec = pltpu.PrefetchScalarGridSpec(
        num_scalar_prefetch=1, grid=(nb, groups), in_specs=in_specs,
        out_specs=pl.BlockSpec((None, nh, hd), lambda b, g, pt_ref: (b, 0, 0)),
        scratch_shapes=[pltpu.VMEM((nh, 1), F32), pltpu.VMEM((nh, 1), F32),
                        pltpu.VMEM((nh, hd), F32), pltpu.VMEM((nh, 1), F32)])
    return pl.pallas_call(
        functools.partial(_attn_sample_kernel, pg=pg, page=page),
        grid_spec=grid_spec,
        out_shape=jax.ShapeDtypeStruct((nb, nh, hd), F32),
        compiler_params=pltpu.CompilerParams(dimension_semantics=("arbitrary", "arbitrary"),
                                             vmem_limit_bytes=VMEM_LIMIT),
        name="attn_sample",
    )(pt, q, kn, vn, ln, eye, at, *([cache_k] * pg), *([cache_v] * pg), *([cache_logf] * pg))


def _suffix_constants(page):
    import numpy as np
    rows = page * N_HEADS
    at = np.zeros((page, rows + LANES), np.float32)
    tok = np.arange(rows) // N_HEADS
    at[:, :rows] = (np.arange(page)[:, None] > tok[None, :]).astype(np.float32)
    at[:, rows:] = 1.0
    return jnp.eye(N_HEADS, dtype=BF16), jnp.asarray(at, BF16)


def _outproj_kernel(h_ref, yc_ref, o_ref, sgc_ref, sga_ref, p_ref, wco_ref, wao_ref, wo_ref,
                    gffn_ref, wgt_ref, wup_ref, wdn_ref, gple_ref, wpg_ref, wpp_ref, gfin_ref,
                    out_ref, *, final):
    h = h_ref[...]
    merged = (sgc_ref[...].astype(F32) * _dot(yc_ref[...], wco_ref[...])
              + sga_ref[...].astype(F32) * _dot(o_ref[...].astype(BF16), wao_ref[...]))
    h = h + _dot(merged.astype(BF16), wo_ref[...])
    hn = _rms(h, gffn_ref[...]).astype(BF16)
    gate = _dot(hn, wgt_ref[...])
    act = (gate * _sigmoid(gate) * _dot(hn, wup_ref[...])).astype(BF16)
    h = h + _dot(act, wdn_ref[...])
    hn = _rms(h, gple_ref[...]).astype(BF16)
    h = h + _sigmoid(_dot(hn, wpg_ref[...])) * _dot(p_ref[...].astype(BF16), wpp_ref[...])
    out_ref[...] = _rms(h, gfin_ref[...]) if final else h


def _outproj(h, yc, o, sgc, sga, p, lw, g_final, tm, final):
    m, d = h.shape
    dple = p.shape[1]
    row = lambda n: pl.BlockSpec((tm, n), lambda i: (i, 0))
    ws = [lw['wco'], lw['wao'], lw['wo'], lw['g_ffn'], lw['wgt'], lw['wup'], lw['wdn'],
          lw['g_ple'], lw['wpg'], lw['wpp'], g_final]
    return pl.pallas_call(
        functools.partial(_outproj_kernel, final=final),
        grid=(m // tm,),
        in_specs=[row(d), row(d), row(d), row(d), row(d), row(dple)] + [_resident(w.shape) for w in ws],
        out_specs=row(d),
        out_shape=jax.ShapeDtypeStruct((m, d), F32),
        compiler_params=pltpu.CompilerParams(dimension_semantics=("arbitrary",),
                                             vmem_limit_bytes=VMEM_LIMIT),
        name="outproj",
    )(h, yc, o, sgc, sga, p, *ws)


TM_IN = 256
TQ = 256
TM_OUT = 256
PAGES_PER_STEP = 4


def _layer_weights(i, w_in, b_forget, conv_w, w_conv_out, w_attn_out, w_o, g_mix, g_ffn, w_gate,
                   w_up, w_down, g_ple, w_ple_gate, w_ple_proj):
    d = w_in.shape[1]
    nh = N_HEADS
    wi = w_in[i]
    main = 6 * d
    return dict(
        wa=wi[:, :main].astype(BF16),
        wfl=jnp.tile(wi[:, main:main + nh], (1, N_PIECES)).astype(BF16),
        wg=wi[:, main + nh:].astype(BF16),
        bf3=jnp.tile(b_forget[i][None, :], (1, N_PIECES)),
        conv_w=conv_w[i], g_mix=g_mix[i][None, :], g_ffn=g_ffn[i][None, :], g_ple=g_ple[i][None, :],
        wco=w_conv_out[i].astype(BF16), wao=w_attn_out[i].astype(BF16), wo=w_o[i].astype(BF16),
        wgt=w_gate[i].astype(BF16), wup=w_up[i].astype(BF16), wdn=w_down[i].astype(BF16),
        wpg=w_ple_gate[i].astype(BF16), wpp=w_ple_proj[i].astype(BF16))


def kernel(x_prompt, x_sample, cache_k, cache_v, cache_logf, state_conv, page_table, p_prompt, p_sample, w_in, b_forget, conv_w, w_conv_out, w_attn_out, w_o, g_mix, g_ffn, w_gate, w_up, w_down, g_ple, w_ple_gate, w_ple_proj, g_final):
    depth = w_in.shape[0]
    b, l, d = x_prompt.shape
    nb = x_sample.shape[0]
    nh, hd = N_HEADS, HEAD_DIM
    tm_in, tq, tm_out = min(TM_IN, l), min(TQ, l), min(TM_OUT, l)
    assert x_sample.shape[1] == 1 and d == nh * hd
    assert l % tm_in == 0 and l % tq == 0 and (b * l) % tm_out == 0
    assert page_table.shape[1] % PAGES_PER_STEP == 0

    tri = jnp.tril(jnp.ones((tm_in, tm_in), F32)).astype(BF16)
    consts = _extras_constants(d) + (tri,)
    eye, at = _suffix_constants(cache_k.shape[2])
    gfin = g_final[None, :]

    hp = x_prompt
    hs = x_sample.reshape(nb, d)
    outs = [[] for _ in range(8)]
    for i in range(depth):
        lw = _layer_weights(i, w_in, b_forget, conv_w, w_conv_out, w_attn_out, w_o, g_mix, g_ffn,
                            w_gate, w_up, w_down, g_ple, w_ple_gate, w_ple_proj)
        final = i == depth - 1

        k32, v32, lf, qa, ka, vb, yc, sgc, sga, cl = _inproj_prompt(hp, lw, consts, tm_in)
        o = _attn_prompt(qa, ka, vb, tq)
        flat = lambda a: a.reshape(b * l, a.shape[-1])
        hp = _outproj(flat(hp), flat(yc), flat(o), flat(sgc), flat(sga), flat(p_prompt[i]), lw, gfin,
                      tm_out, final).reshape(b, l, d)

        ks, vs, lfs, qs, us, ycs, sgcs, sgas = _inproj_sample(
            hs, state_conv[i, :, 0, :], state_conv[i, :, 1, :], lw)
        heads = lambda a: a.reshape(nb, nh, hd)
        os_ = _attn_sample(i, page_table, heads(qs), heads(ks), heads(vs), lfs.reshape(nb, nh, 1),
                           cache_k, cache_v, cache_logf, eye, at, PAGES_PER_STEP)
        hs = _outproj(hs, ycs, os_.reshape(nb, d), sgcs, sgas, p_sample[i].reshape(nb, -1), lw, gfin,
                      nb, final)

        outs[0].append(k32.reshape(b, l, nh, hd))
        outs[1].append(v32.reshape(b, l, nh, hd))
        outs[2].append(lf)
        outs[3].append(cl)
        outs[4].append(ks.reshape(nb, 1, nh, hd))
        outs[5].append(vs.reshape(nb, 1, nh, hd))
        outs[6].append(lfs.reshape(nb, 1, nh))
        outs[7].append(jnp.stack([state_conv[i, :, 1, :], us], axis=1))
    return (hp, hs.reshape(nb, 1, d)) + tuple(jnp.stack(o_) for o_ in outs)
```

```python
import functools

import numpy as np
import jax
import jax.numpy as jnp
from jax import lax
from jax.experimental import pallas as pl
from jax.experimental.pallas import tpu as pltpu

N_HEADS = 16
HEAD_DIM = 64
CONV_WIDTH = 3
RMS_EPS = 1e-6
LANES = 128
SUBLANES = 8
LOG2E = 1.4426950408889634
N_PIECES = 3
VMEM_LIMIT = 56 * 1024 * 1024

F32 = jnp.float32
BF16 = jnp.bfloat16


def _dot(a, b):
    return jnp.dot(a, b, preferred_element_type=F32)


def _dot_nt(a, b):
    return lax.dot_general(a, b, (((1,), (1,)), ((), ())), preferred_element_type=F32)


def _rms(x, g):
    return x * lax.rsqrt(jnp.mean(x * x, axis=-1, keepdims=True) + RMS_EPS) * g


def _sigmoid(x):
    return 1.0 / (1.0 + jnp.exp(-x))


def _log_sigmoid(x):
    return jnp.minimum(x, 0.0) - jnp.log1p(jnp.exp(-jnp.abs(x)))


def _split3(x):
    p1 = x.astype(BF16).astype(F32)
    r = x - p1
    p2 = r.astype(BF16).astype(F32)
    p3 = (r - p2).astype(BF16).astype(F32)
    return p1, p2, p3


def _resident(shape):
    nd = len(shape)
    return pl.BlockSpec(shape, lambda *_: (0,) * nd, pipeline_mode=pl.Buffered(1))


def _inproj_prompt_kernel(h_ref, g_ref, wa_ref, wfl_ref, wg_ref, bf_ref, cw_ref, pq_ref, pk_ref,
                          oq_ref, ok_ref, tri_ref,
                          k32_ref, v32_ref, lf_ref, qa_ref, ka_ref, va_ref, yc_ref, sgc_ref,
                          sga_ref, cl_ref,
                          ubuf, ccar, *, tm, d):
    nh = N_HEADS

    @pl.when(pl.program_id(1) == 0)
    def _():
        ubuf[0:SUBLANES, :] = jnp.zeros((SUBLANES, d), F32)
        ccar[...] = jnp.zeros_like(ccar)

    hn = _rms(h_ref[...], g_ref[...]).astype(BF16)

    bg = _dot(hn, wa_ref[:, 0:d])
    u = _dot(hn, wa_ref[:, d:2 * d]) * _dot(hn, wa_ref[:, 2 * d:3 * d])
    ubuf[SUBLANES:SUBLANES + tm, :] = u
    um1 = ubuf[SUBLANES - 1:SUBLANES - 1 + tm, :]
    um2 = ubuf[SUBLANES - 2:SUBLANES - 2 + tm, :]
    conv = cw_ref[0:1, :] * um2 + cw_ref[1:2, :] * um1 + cw_ref[2:3, :] * u
    yc_ref[...] = (bg * conv).astype(BF16)
    ubuf[0:SUBLANES, :] = u[tm - SUBLANES:tm, :]
    cl_ref[...] = u[tm - (CONV_WIDTH - 1):tm, :]

    lf3 = _log_sigmoid(_dot(hn, wfl_ref[...]) + bf_ref[...])
    lf_ref[...] = lf3[:, 0:nh]
    l1, l2, l3 = _split3(lf3)
    tri = tri_ref[...]
    cum = (_dot(tri, l1.astype(BF16)) + _dot(tri, l2.astype(BF16)) + _dot(tri, l3.astype(BF16))
           + ccar[...])
    ccar[...] = cum[tm - 1:tm, :]
    c1, c2, c3 = _split3(cum * LOG2E)
    lane3 = lax.broadcasted_iota(jnp.int32, cum.shape, 1)
    cpieces = jnp.where(lane3 < nh, c1, jnp.where(lane3 < 2 * nh, c2, c3)).astype(BF16)
    xq = _dot(cpieces, pq_ref[...]) + oq_ref[...]
    xk = _dot(cpieces, pk_ref[...]) + ok_ref[...]

    q = _dot(hn, wa_ref[:, 3 * d:4 * d]) * (HEAD_DIM ** -0.5 * LOG2E)
    k = _dot(hn, wa_ref[:, 4 * d:5 * d])
    k32_ref[...] = k
    v = _dot(hn, wa_ref[:, 5 * d:6 * d])
    v32_ref[...] = v
    low = lax.broadcasted_iota(jnp.int32, (tm, LANES), 1) < HEAD_DIM
    for p in range(nh // 2):
        sl = slice(p * LANES, (p + 1) * LANES)
        ev = slice(2 * p * LANES, (2 * p + 1) * LANES)
        od = slice((2 * p + 1) * LANES, (2 * p + 2) * LANES)
        qa_ref[:, ev] = jnp.where(low, q[:, sl], xq[:, sl]).astype(BF16)
        qa_ref[:, od] = jnp.where(low, xq[:, sl], q[:, sl]).astype(BF16)
        ka_ref[:, ev] = jnp.where(low, k[:, sl], xk[:, sl]).astype(BF16)
        ka_ref[:, od] = jnp.where(low, xk[:, sl], k[:, sl]).astype(BF16)
        va_ref[:, ev] = jnp.where(low, v[:, sl], 1.0).astype(BF16)
        va_ref[:, od] = jnp.where(low, 1.0, v[:, sl]).astype(BF16)

    sgc_ref[...] = _sigmoid(_dot(hn, wg_ref[:, 0:d])).astype(BF16)
    sga_ref[...] = _sigmoid(_dot(hn, wg_ref[:, d:2 * d])).astype(BF16)


def _inproj_sample_kernel(h_ref, g_ref, wa_ref, wfl_ref, wg_ref, bf_ref, cw_ref, um2_ref, um1_ref,
                          k32_ref, v32_ref, lf_ref, q_ref, u_ref, yc_ref, sgc_ref, sga_ref, *, d):
    nh = N_HEADS
    hn = _rms(h_ref[...], g_ref[...]).astype(BF16)
    bg = _dot(hn, wa_ref[:, 0:d])
    u = _dot(hn, wa_ref[:, d:2 * d]) * _dot(hn, wa_ref[:, 2 * d:3 * d])
    conv = cw_ref[0:1, :] * um2_ref[...] + cw_ref[1:2, :] * um1_ref[...] + cw_ref[2:3, :] * u
    yc_ref[...] = (bg * conv).astype(BF16)
    u_ref[...] = u
    lf3 = _log_sigmoid(_dot(hn, wfl_ref[...]) + bf_ref[...])
    lf_ref[...] = lf3[:, 0:nh]
    q_ref[...] = _dot(hn, wa_ref[:, 3 * d:4 * d]) * (HEAD_DIM ** -0.5)
    k32_ref[...] = _dot(hn, wa_ref[:, 4 * d:5 * d])
    v32_ref[...] = _dot(hn, wa_ref[:, 5 * d:6 * d])
    sgc_ref[...] = _sigmoid(_dot(hn, wg_ref[:, 0:d])).astype(BF16)
    sga_ref[...] = _sigmoid(_dot(hn, wg_ref[:, d:2 * d])).astype(BF16)


def _extras_constants(d):
    nh = N_HEADS
    pq = np.zeros((N_PIECES * nh, d), np.float32)
    pk = np.zeros((N_PIECES * nh, d), np.float32)
    oq = np.zeros((1, d), np.float32)
    ok = np.zeros((1, d), np.float32)
    for h in range(nh):
        base = (h // 2) * LANES + (HEAD_DIM if h % 2 == 0 else 0)
        for j in range(N_PIECES):
            pq[j * nh + h, base + j] = 1.0
            pk[j * nh + h, base + N_PIECES + j] = -1.0
            oq[0, base + N_PIECES + j] = 1.0
            ok[0, base + j] = 1.0
    return (jnp.asarray(pq, BF16), jnp.asarray(pk, BF16), jnp.asarray(oq), jnp.asarray(ok))


def _inproj_prompt(h, lw, consts, tm):
    b, l, d = h.shape
    nh = N_HEADS
    pq, pk, oq, ok, tri = consts
    grid = (b, l // tm)
    row = lambda n: pl.BlockSpec((None, tm, n), lambda i, j: (i, j, 0))
    in_specs = [row(d), _resident((1, d)), _resident(lw['wa'].shape), _resident(lw['wfl'].shape),
                _resident(lw['wg'].shape), _resident((1, N_PIECES * nh)),
                _resident((CONV_WIDTH, d)), _resident(pq.shape), _resident(pk.shape),
                _resident(oq.shape), _resident(ok.shape), _resident(tri.shape)]
    act = lambda n, dt: jax.ShapeDtypeStruct((b, l, n), dt)
    out_shape = [act(d, F32), act(d, F32), act(nh, F32), act(2 * d, BF16), act(2 * d, BF16),
                 act(2 * d, BF16), act(d, BF16), act(d, BF16), act(d, BF16),
                 jax.ShapeDtypeStruct((b, CONV_WIDTH - 1, d), F32)]
    out_specs = [row(d), row(d), row(nh), row(2 * d), row(2 * d), row(2 * d), row(d), row(d), row(d),
                 pl.BlockSpec((None, CONV_WIDTH - 1, d), lambda i, j: (i, 0, 0))]
    return pl.pallas_call(
        functools.partial(_inproj_prompt_kernel, tm=tm, d=d),
        grid=grid, in_specs=in_specs, out_specs=out_specs, out_shape=out_shape,
        scratch_shapes=[pltpu.VMEM((tm + SUBLANES, d), F32), pltpu.VMEM((1, N_PIECES * nh), F32)],
        compiler_params=pltpu.CompilerParams(dimension_semantics=("arbitrary", "arbitrary"),
                                             vmem_limit_bytes=VMEM_LIMIT),
        name="inproj_prompt",
    )(h, lw['g_mix'], lw['wa'], lw['wfl'], lw['wg'], lw['bf3'], lw['conv_w'], pq, pk, oq, ok, tri)


def _inproj_sample(h, um2, um1, lw):
    m, d = h.shape
    nh = N_HEADS
    full = lambda n, dt: jax.ShapeDtypeStruct((m, n), dt)
    out_shape = [full(d, F32), full(d, F32), full(nh, F32), full(d, F32), full(d, F32),
                 full(d, BF16), full(d, BF16), full(d, BF16)]
    return pl.pallas_call(
        functools.partial(_inproj_sample_kernel, d=d),
        out_shape=out_shape,
        compiler_params=pltpu.CompilerParams(vmem_limit_bytes=VMEM_LIMIT),
        name="inproj_sample",
    )(h, lw['g_mix'], lw['wa'], lw['wfl'], lw['wg'], lw['bf3'], lw['conv_w'], um2, um1)


def _attn_prompt_kernel(qa_ref, ka_ref, va_ref, o_ref, *, tq, nq, nsplit):
    qi = pl.program_id(2)
    rq = tq // nsplit
    low = lax.broadcasted_iota(jnp.int32, (rq, LANES), 1) < HEAD_DIM
    units = [(r, slice(par * LANES, (par + 1) * LANES)) for r in range(nsplit) for par in range(2)]

    causal = (lax.broadcasted_iota(jnp.int32, (rq, rq), 1)
              <= lax.broadcasted_iota(jnp.int32, (rq, rq), 0))

    def scores(r, hs, past):
        nv = past + r * rq
        q = qa_ref[r * rq:(r + 1) * rq, hs]
        out = [jnp.where(causal, _dot_nt(q, ka_ref[nv:nv + rq, hs]), -jnp.inf)]
        if nv:
            out.append(_dot_nt(q, ka_ref[0:nv, hs]))
        return out

    def probs(ss):
        m = jnp.max(ss[0], axis=1, keepdims=True)
        for s in ss[1:]:
            m = jnp.maximum(m, jnp.max(s, axis=1, keepdims=True))
        return [jnp.exp2(s - m).astype(BF16) for s in ss]

    def values(ps, hs):
        nv = ps[1].shape[1] if len(ps) > 1 else 0
        acc = _dot(ps[0], va_ref[nv:nv + rq, hs])
        if nv:
            acc = acc + _dot(ps[1], va_ref[0:nv, hs])
        return acc

    for c in range(nq):
        @pl.when(qi == c)
        def _(c=c):
            past = c * tq
            s, p, acc = {}, {}, {}
            for t in range(len(units) + 2):
                if t < len(units):
                    s[t] = scores(*units[t], past)
                if 1 <= t <= len(units):
                    p[t - 1] = probs(s.pop(t - 1))
                if t >= 2:
                    acc[t - 2] = values(p.pop(t - 2), units[t - 2][1])
            for r in range(nsplit):
                a_e, a_o = acc[2 * r], acc[2 * r + 1]
                o_e = a_e / a_e[:, HEAD_DIM:HEAD_DIM + 1]
                o_o = a_o / a_o[:, 0:1]
                o_ref[r * rq:(r + 1) * rq, :] = jnp.where(low, o_e, o_o).astype(o_ref.dtype)


def _attn_prompt(qa, ka, va, tq):
    b, l, d2 = qa.shape
    npair = N_HEADS // 2
    return pl.pallas_call(
        functools.partial(_attn_prompt_kernel, tq=tq, nq=l // tq, nsplit=ATTN_ROW_PARTS),
        grid=(b, npair, l // tq),
        in_specs=[pl.BlockSpec((None, tq, 2 * LANES), lambda i, p, j: (i, j, p)),
                  pl.BlockSpec((None, l, 2 * LANES), lambda i, p, j: (i, 0, p)),
                  pl.BlockSpec((None, l, 2 * LANES), lambda i, p, j: (i, 0, p))],
        out_specs=pl.BlockSpec((None, tq, LANES), lambda i, p, j: (i, j, p)),
        out_shape=jax.ShapeDtypeStruct((b, l, d2 // 2), BF16),
        compiler_params=pltpu.CompilerParams(
            dimension_semantics=("arbitrary", "arbitrary", "arbitrary"),
            vmem_limit_bytes=VMEM_LIMIT),
        name="attn_prompt",
    )(qa, ka, va)


def _attn_sample_kernel(pt_ref, q_ref, kn_ref, vn_ref, ln_ref, rep_ref, ut_ref, *refs, npg, page):
    del pt_ref
    k_refs = refs[0:npg]
    v_refs = refs[npg:2 * npg]
    lf_refs = refs[2 * npg:3 * npg]
    o_ref = refs[3 * npg]
    nh = N_HEADS
    d = nh * HEAD_DIM

    own = (lax.broadcasted_iota(jnp.int32, (nh, d), 1) // HEAD_DIM
           == lax.broadcasted_iota(jnp.int32, (nh, d), 0))
    qrep = _dot(q_ref[...].astype(BF16), rep_ref[...])
    qbd = jnp.where(own, qrep, 0.0).astype(BF16)

    scores = [_dot(qbd, k_refs[p][...].reshape(d, page).astype(BF16)) for p in range(npg)]

    pieces = []
    for p in range(npg):
        pieces.extend(_split3(lf_refs[p][...]))
    sfx_all = _dot(jnp.concatenate(pieces, axis=0).astype(BF16), ut_ref[...])
    car = ln_ref[...]
    z = [None] * npg
    for p in reversed(range(npg)):
        blk = sfx_all[p * N_PIECES * nh:(p + 1) * N_PIECES * nh]
        sfx = blk[0:nh] + blk[nh:2 * nh] + blk[2 * nh:3 * nh]
        z[p] = scores[p] + sfx[:, 0:page] + car
        car = car + sfx[:, page:page + 1]

    s_new = jnp.sum(jnp.where(own, qrep * kn_ref[...], 0.0), axis=1, keepdims=True)
    m = s_new
    for p in range(npg):
        m = jnp.maximum(m, jnp.max(z[p], axis=1, keepdims=True))
    e_new = jnp.exp(s_new - m)
    l = e_new
    acc = e_new * vn_ref[...]
    for p in range(npg):
        e = jnp.exp(z[p] - m)
        l = l + jnp.sum(e, axis=1, keepdims=True)
        acc = acc + _dot_nt(e.astype(BF16), v_refs[p][...].reshape(d, page).astype(BF16))
    o_ref[...] = jnp.sum(jnp.where(own, acc / l, 0.0), axis=0, keepdims=True)


def _attn_sample(layer, page_table, q, kn, vn, ln, ck, cv, clf, rep, ut):
    nb, nh, hd = q.shape
    d = nh * hd
    page = ck.shape[-1]
    npg = page_table.shape[1]
    pt = page_table.reshape(-1)

    def page_map(p, nd):
        return lambda b, pt_ref: (layer, pt_ref[b * npg + p]) + (0,) * nd

    const = lambda a: pl.BlockSpec(a.shape, lambda b, pt_ref: (0,) * a.ndim)
    in_specs = ([pl.BlockSpec((None, nh, hd), lambda b, pt_ref: (b, 0, 0)),
                 pl.BlockSpec((None, 1, d), lambda b, pt_ref: (b, 0, 0)),
                 pl.BlockSpec((None, 1, d), lambda b, pt_ref: (b, 0, 0)),
                 pl.BlockSpec((None, nh, 1), lambda b, pt_ref: (b, 0, 0)),
                 const(rep), const(ut)]
                + [pl.BlockSpec((None, None, nh, hd, page), page_map(p, 3)) for p in range(npg)]
                + [pl.BlockSpec((None, None, nh, hd, page), page_map(p, 3)) for p in range(npg)]
                + [pl.BlockSpec((None, None, nh, page), page_map(p, 2)) for p in range(npg)])
    grid_spec = pltpu.PrefetchScalarGridSpec(
        num_scalar_prefetch=1, grid=(nb,), in_specs=in_specs,
        out_specs=pl.BlockSpec((None, 1, d), lambda b, pt_ref: (b, 0, 0)))
    return pl.pallas_call(
        functools.partial(_attn_sample_kernel, npg=npg, page=page),
        grid_spec=grid_spec,
        out_shape=jax.ShapeDtypeStruct((nb, 1, d), F32),
        compiler_params=pltpu.CompilerParams(dimension_semantics=("arbitrary",),
                                             vmem_limit_bytes=VMEM_LIMIT),
        name="attn_sample",
    )(pt, q, kn, vn, ln, rep, ut, *([ck] * npg), *([cv] * npg), *([clf] * npg))


def _sample_constants(page):
    rep = np.tile(np.eye(HEAD_DIM, dtype=np.float32), (1, N_HEADS))
    ut = np.ones((page, 2 * page), np.float32)
    ut[:, :page] = np.arange(page)[:, None] > np.arange(page)[None, :]
    return jnp.asarray(rep, BF16), jnp.asarray(ut, BF16)


def _outproj_kernel(h_ref, yc_ref, o_ref, sgc_ref, sga_ref, p_ref, wco_ref, wao_ref, wo_ref,
                    gffn_ref, wgt_ref, wup_ref, wdn_ref, gple_ref, wpg_ref, wpp_ref, gfin_ref,
                    out_ref, *, final):
    h = h_ref[...]
    merged = (sgc_ref[...].astype(F32) * _dot(yc_ref[...], wco_ref[...])
              + sga_ref[...].astype(F32) * _dot(o_ref[...].astype(BF16), wao_ref[...]))
    h = h + _dot(merged.astype(BF16), wo_ref[...])
    hn = _rms(h, gffn_ref[...]).astype(BF16)
    gate = _dot(hn, wgt_ref[...])
    act = (gate * _sigmoid(gate) * _dot(hn, wup_ref[...])).astype(BF16)
    h = h + _dot(act, wdn_ref[...])
    hn = _rms(h, gple_ref[...]).astype(BF16)
    h = h + _sigmoid(_dot(hn, wpg_ref[...])) * _dot(p_ref[...].astype(BF16), wpp_ref[...])
    out_ref[...] = _rms(h, gfin_ref[...]) if final else h


def _outproj(h, yc, o, sgc, sga, p, lw, g_final, tm, final):
    m, d = h.shape
    dple = p.shape[1]
    row = lambda n: pl.BlockSpec((tm, n), lambda i: (i, 0))
    ws = [lw['wco'], lw['wao'], lw['wo'], lw['g_ffn'], lw['wgt'], lw['wup'], lw['wdn'],
          lw['g_ple'], lw['wpg'], lw['wpp'], g_final]
    return pl.pallas_call(
        functools.partial(_outproj_kernel, final=final),
        grid=(m // tm,),
        in_specs=[row(d), row(d), row(d), row(d), row(d), row(dple)] + [_resident(w.shape) for w in ws],
        out_specs=row(d),
        out_shape=jax.ShapeDtypeStruct((m, d), F32),
        compiler_params=pltpu.CompilerParams(dimension_semantics=("arbitrary",),
                                             vmem_limit_bytes=VMEM_LIMIT),
        name="outproj",
    )(h, yc, o, sgc, sga, p, *ws)


TM_IN = 256
TQ = 1024
ATTN_ROW_PARTS = 4
TM_OUT = 256


def _layer_weights(i, w_in, b_forget, conv_w, w_conv_out, w_attn_out, w_o, g_mix, g_ffn, w_gate,
                   w_up, w_down, g_ple, w_ple_gate, w_ple_proj):
    d = w_in.shape[1]
    nh = N_HEADS
    wi = w_in[i]
    main = 6 * d
    return dict(
        wa=wi[:, :main].astype(BF16),
        wfl=jnp.tile(wi[:, main:main + nh], (1, N_PIECES)).astype(BF16),
        wg=wi[:, main + nh:].astype(BF16),
        bf3=jnp.tile(b_forget[i][None, :], (1, N_PIECES)),
        conv_w=conv_w[i], g_mix=g_mix[i][None, :], g_ffn=g_ffn[i][None, :], g_ple=g_ple[i][None, :],
        wco=w_conv_out[i].astype(BF16), wao=w_attn_out[i].astype(BF16), wo=w_o[i].astype(BF16),
        wgt=w_gate[i].astype(BF16), wup=w_up[i].astype(BF16), wdn=w_down[i].astype(BF16),
        wpg=w_ple_gate[i].astype(BF16), wpp=w_ple_proj[i].astype(BF16))


def kernel(x_prompt, x_sample, cache_k, cache_v, cache_logf, state_conv, page_table, p_prompt, p_sample, w_in, b_forget, conv_w, w_conv_out, w_attn_out, w_o, g_mix, g_ffn, w_gate, w_up, w_down, g_ple, w_ple_gate, w_ple_proj, g_final):
    depth = w_in.shape[0]
    b, l, d = x_prompt.shape
    nb = x_sample.shape[0]
    nh, hd = N_HEADS, HEAD_DIM
    tm_in, tq, tm_out = min(TM_IN, l), min(TQ, l), min(TM_OUT, l)
    assert x_sample.shape[1] == 1 and d == nh * hd
    assert l % tm_in == 0 and l % tq == 0 and (b * l) % tm_out == 0

    tri = jnp.tril(jnp.ones((tm_in, tm_in), F32)).astype(BF16)
    consts = _extras_constants(d) + (tri,)
    rep, ut = _sample_constants(cache_k.shape[2])
    gfin = g_final[None, :]
    ck = jnp.transpose(cache_k, (0, 1, 3, 4, 2))
    cv = jnp.transpose(cache_v, (0, 1, 3, 4, 2))
    clf = jnp.transpose(cache_logf, (0, 1, 3, 2))

    hp = x_prompt
    hs = x_sample.reshape(nb, d)
    outs = [[] for _ in range(8)]
    for i in range(depth):
        lw = _layer_weights(i, w_in, b_forget, conv_w, w_conv_out, w_attn_out, w_o, g_mix, g_ffn,
                            w_gate, w_up, w_down, g_ple, w_ple_gate, w_ple_proj)
        final = i == depth - 1

        k32, v32, lf, qa, ka, va, yc, sgc, sga, cl = _inproj_prompt(hp, lw, consts, tm_in)
        o = _attn_prompt(qa, ka, va, tq)
        flat = lambda a: a.reshape(b * l, a.shape[-1])
        hp = _outproj(flat(hp), flat(yc), flat(o), flat(sgc), flat(sga), flat(p_prompt[i]), lw, gfin,
                      tm_out, final).reshape(b, l, d)

        ks, vs, lfs, qs, us, ycs, sgcs, sgas = _inproj_sample(
            hs, state_conv[i, :, 0, :], state_conv[i, :, 1, :], lw)
        os_ = _attn_sample(i, page_table, qs.reshape(nb, nh, hd), ks.reshape(nb, 1, d),
                           vs.reshape(nb, 1, d), lfs.reshape(nb, nh, 1), ck, cv, clf, rep, ut)
        hs = _outproj(hs, ycs, os_.reshape(nb, d), sgcs, sgas, p_sample[i].reshape(nb, -1), lw, gfin,
                      nb, final)

        outs[0].append(k32.reshape(b, l, nh, hd))
        outs[1].append(v32.reshape(b, l, nh, hd))
        outs[2].append(lf)
        outs[3].append(cl)
        outs[4].append(ks.reshape(nb, 1, nh, hd))
        outs[5].append(vs.reshape(nb, 1, nh, hd))
        outs[6].append(lfs.reshape(nb, 1, nh))
        outs[7].append(jnp.stack([state_conv[i, :, 1, :], us], axis=1))
    return (hp, hs.reshape(nb, 1, d)) + tuple(jnp.stack(o_) for o_ in outs)
```

```python
import functools

import numpy as np
import jax
import jax.numpy as jnp
from jax import lax
from jax.experimental import pallas as pl
from jax.experimental.pallas import tpu as pltpu

N_HEADS = 16
HEAD_DIM = 64
CONV_WIDTH = 3
RMS_EPS = 1e-6
LANES = 128
SUBLANES = 8
LOG2E = 1.4426950408889634
N_PIECES = 3
VMEM_LIMIT = 56 * 1024 * 1024

F32 = jnp.float32
BF16 = jnp.bfloat16


def _dot(a, b):
    return jnp.dot(a, b, preferred_element_type=F32)


def _dot_nt(a, b):
    return lax.dot_general(a, b, (((1,), (1,)), ((), ())), preferred_element_type=F32)


def _rms(x, g):
    return x * lax.rsqrt(jnp.mean(x * x, axis=-1, keepdims=True) + RMS_EPS) * g


def _sigmoid(x):
    return 1.0 / (1.0 + jnp.exp(-x))


def _log_sigmoid(x):
    return jnp.minimum(x, 0.0) - jnp.log1p(jnp.exp(-jnp.abs(x)))


def _split3(x):
    p1 = x.astype(BF16).astype(F32)
    r = x - p1
    p2 = r.astype(BF16).astype(F32)
    p3 = (r - p2).astype(BF16).astype(F32)
    return p1, p2, p3


def _resident(shape):
    nd = len(shape)
    return pl.BlockSpec(shape, lambda *_: (0,) * nd, pipeline_mode=pl.Buffered(1))


def _inproj_prompt_kernel(h_ref, g_ref, wa_ref, wfl_ref, wg_ref, bf_ref, cw_ref, pq_ref, pk_ref,
                          oq_ref, ok_ref, tri_ref,
                          k32_ref, v32_ref, lf_ref, qa_ref, ka_ref, va_ref, yc_ref, sgc_ref,
                          sga_ref, cl_ref,
                          ubuf, ccar, *, tm, d):
    nh = N_HEADS

    @pl.when(pl.program_id(1) == 0)
    def _():
        ubuf[0:SUBLANES, :] = jnp.zeros((SUBLANES, d), F32)
        ccar[...] = jnp.zeros_like(ccar)

    hn = _rms(h_ref[...], g_ref[...]).astype(BF16)

    bg = _dot(hn, wa_ref[:, 0:d])
    u = _dot(hn, wa_ref[:, d:2 * d]) * _dot(hn, wa_ref[:, 2 * d:3 * d])
    ubuf[SUBLANES:SUBLANES + tm, :] = u
    um1 = ubuf[SUBLANES - 1:SUBLANES - 1 + tm, :]
    um2 = ubuf[SUBLANES - 2:SUBLANES - 2 + tm, :]
    conv = cw_ref[0:1, :] * um2 + cw_ref[1:2, :] * um1 + cw_ref[2:3, :] * u
    yc_ref[...] = (bg * conv).astype(BF16)
    ubuf[0:SUBLANES, :] = u[tm - SUBLANES:tm, :]
    cl_ref[...] = u[tm - (CONV_WIDTH - 1):tm, :]

    lf3 = _log_sigmoid(_dot(hn, wfl_ref[...]) + bf_ref[...])
    lf_ref[...] = lf3[:, 0:nh]
    l1, l2, l3 = _split3(lf3)
    tri = tri_ref[...]
    cum = (_dot(tri, l1.astype(BF16)) + _dot(tri, l2.astype(BF16)) + _dot(tri, l3.astype(BF16))
           + ccar[...])
    ccar[...] = cum[tm - 1:tm, :]
    c1, c2, c3 = _split3(cum * LOG2E)
    lane3 = lax.broadcasted_iota(jnp.int32, cum.shape, 1)
    cpieces = jnp.where(lane3 < nh, c1, jnp.where(lane3 < 2 * nh, c2, c3)).astype(BF16)
    xq = _dot(cpieces, pq_ref[...]) + oq_ref[...]
    xk = _dot(cpieces, pk_ref[...]) + ok_ref[...]

    q = _dot(hn, wa_ref[:, 3 * d:4 * d]) * (HEAD_DIM ** -0.5 * LOG2E)
    k = _dot(hn, wa_ref[:, 4 * d:5 * d])
    k32_ref[...] = k
    v = _dot(hn, wa_ref[:, 5 * d:6 * d])
    v32_ref[...] = v
    low = lax.broadcasted_iota(jnp.int32, (tm, LANES), 1) < HEAD_DIM
    for p in range(nh // 2):
        sl = slice(p * LANES, (p + 1) * LANES)
        ev = slice(2 * p * LANES, (2 * p + 1) * LANES)
        od = slice((2 * p + 1) * LANES, (2 * p + 2) * LANES)
        qa_ref[:, ev] = jnp.where(low, q[:, sl], xq[:, sl]).astype(BF16)
        qa_ref[:, od] = jnp.where(low, xq[:, sl], q[:, sl]).astype(BF16)
        ka_ref[:, ev] = jnp.where(low, k[:, sl], xk[:, sl]).astype(BF16)
        ka_ref[:, od] = jnp.where(low, xk[:, sl], k[:, sl]).astype(BF16)
        va_ref[:, ev] = jnp.where(low, v[:, sl], 1.0).astype(BF16)
        va_ref[:, od] = jnp.where(low, 1.0, v[:, sl]).astype(BF16)

    sgc_ref[...] = _sigmoid(_dot(hn, wg_ref[:, 0:d])).astype(BF16)
    sga_ref[...] = _sigmoid(_dot(hn, wg_ref[:, d:2 * d])).astype(BF16)


def _inproj_sample_kernel(h_ref, g_ref, wa_ref, wfl_ref, wg_ref, bf_ref, cw_ref, um2_ref, um1_ref,
                          k32_ref, v32_ref, lf_ref, q_ref, u_ref, yc_ref, sgc_ref, sga_ref, *, d):
    nh = N_HEADS
    hn = _rms(h_ref[...], g_ref[...]).astype(BF16)
    bg = _dot(hn, wa_ref[:, 0:d])
    u = _dot(hn, wa_ref[:, d:2 * d]) * _dot(hn, wa_ref[:, 2 * d:3 * d])
    conv = cw_ref[0:1, :] * um2_ref[...] + cw_ref[1:2, :] * um1_ref[...] + cw_ref[2:3, :] * u
    yc_ref[...] = (bg * conv).astype(BF16)
    u_ref[...] = u
    lf3 = _log_sigmoid(_dot(hn, wfl_ref[...]) + bf_ref[...])
    lf_ref[...] = lf3[:, 0:nh]
    q_ref[...] = _dot(hn, wa_ref[:, 3 * d:4 * d]) * (HEAD_DIM ** -0.5)
    k32_ref[...] = _dot(hn, wa_ref[:, 4 * d:5 * d])
    v32_ref[...] = _dot(hn, wa_ref[:, 5 * d:6 * d])
    sgc_ref[...] = _sigmoid(_dot(hn, wg_ref[:, 0:d])).astype(BF16)
    sga_ref[...] = _sigmoid(_dot(hn, wg_ref[:, d:2 * d])).astype(BF16)


def _extras_constants(d):
    nh = N_HEADS
    pq = np.zeros((N_PIECES * nh, d), np.float32)
    pk = np.zeros((N_PIECES * nh, d), np.float32)
    oq = np.zeros((1, d), np.float32)
    ok = np.zeros((1, d), np.float32)
    for h in range(nh):
        base = (h // 2) * LANES + (HEAD_DIM if h % 2 == 0 else 0)
        for j in range(N_PIECES):
            pq[j * nh + h, base + j] = 1.0
            pk[j * nh + h, base + N_PIECES + j] = -1.0
            oq[0, base + N_PIECES + j] = 1.0
            ok[0, base + j] = 1.0
    return (jnp.asarray(pq, BF16), jnp.asarray(pk, BF16), jnp.asarray(oq), jnp.asarray(ok))


def _inproj_prompt(h, lw, consts, tm):
    b, l, d = h.shape
    nh = N_HEADS
    pq, pk, oq, ok, tri = consts
    grid = (b, l // tm)
    row = lambda n: pl.BlockSpec((None, tm, n), lambda i, j: (i, j, 0))
    in_specs = [row(d), _resident((1, d)), _resident(lw['wa'].shape), _resident(lw['wfl'].shape),
                _resident(lw['wg'].shape), _resident((1, N_PIECES * nh)),
                _resident((CONV_WIDTH, d)), _resident(pq.shape), _resident(pk.shape),
                _resident(oq.shape), _resident(ok.shape), _resident(tri.shape)]
    act = lambda n, dt: jax.ShapeDtypeStruct((b, l, n), dt)
    out_shape = [act(d, F32), act(d, F32), act(nh, F32), act(2 * d, BF16), act(2 * d, BF16),
                 act(2 * d, BF16), act(d, BF16), act(d, BF16), act(d, BF16),
                 jax.ShapeDtypeStruct((b, CONV_WIDTH - 1, d), F32)]
    out_specs = [row(d), row(d), row(nh), row(2 * d), row(2 * d), row(2 * d), row(d), row(d), row(d),
                 pl.BlockSpec((None, CONV_WIDTH - 1, d), lambda i, j: (i, 0, 0))]
    return pl.pallas_call(
        functools.partial(_inproj_prompt_kernel, tm=tm, d=d),
        grid=grid, in_specs=in_specs, out_specs=out_specs, out_shape=out_shape,
        scratch_shapes=[pltpu.VMEM((tm + SUBLANES, d), F32), pltpu.VMEM((1, N_PIECES * nh), F32)],
        compiler_params=pltpu.CompilerParams(dimension_semantics=("arbitrary", "arbitrary"),
                                             vmem_limit_bytes=VMEM_LIMIT),
        name="inproj_prompt",
    )(h, lw['g_mix'], lw['wa'], lw['wfl'], lw['wg'], lw['bf3'], lw['conv_w'], pq, pk, oq, ok, tri)


def _inproj_sample(h, um2, um1, lw):
    m, d = h.shape
    nh = N_HEADS
    full = lambda n, dt: jax.ShapeDtypeStruct((m, n), dt)
    out_shape = [full(d, F32), full(d, F32), full(nh, F32), full(d, F32), full(d, F32),
                 full(d, BF16), full(d, BF16), full(d, BF16)]
    return pl.pallas_call(
        functools.partial(_inproj_sample_kernel, d=d),
        out_shape=out_shape,
        compiler_params=pltpu.CompilerParams(vmem_limit_bytes=VMEM_LIMIT),
        name="inproj_sample",
    )(h, lw['g_mix'], lw['wa'], lw['wfl'], lw['wg'], lw['bf3'], lw['conv_w'], um2, um1)


def _attention_kernel(pt_ref, qa_ref, ka_ref, va_ref, qs_ref, kn_ref, vn_ref, ln_ref, rep_ref, ut_ref,
                      *refs, tq, nq, nsplit, npg):
    del pt_ref
    k_refs, v_refs, lf_refs = refs[0:npg], refs[npg:2 * npg], refs[2 * npg:3 * npg]
    o_ref, os_ref = refs[3 * npg:]
    qi = pl.program_id(2)
    rq = tq // nsplit
    low = lax.broadcasted_iota(jnp.int32, (rq, LANES), 1) < HEAD_DIM
    units = [(r, slice(par * LANES, (par + 1) * LANES)) for r in range(nsplit) for par in range(2)]

    causal = (lax.broadcasted_iota(jnp.int32, (rq, rq), 1)
              <= lax.broadcasted_iota(jnp.int32, (rq, rq), 0))

    def scores(r, hs, past):
        nv = past + r * rq
        q = qa_ref[r * rq:(r + 1) * rq, hs]
        out = [jnp.where(causal, _dot_nt(q, ka_ref[nv:nv + rq, hs]), -jnp.inf)]
        if nv:
            out.append(_dot_nt(q, ka_ref[0:nv, hs]))
        return out

    def probs(ss):
        m = jnp.max(ss[0], axis=1, keepdims=True)
        for s in ss[1:]:
            m = jnp.maximum(m, jnp.max(s, axis=1, keepdims=True))
        return [jnp.exp2(s - m).astype(BF16) for s in ss]

    def values(ps, hs):
        nv = ps[1].shape[1] if len(ps) > 1 else 0
        acc = _dot(ps[0], va_ref[nv:nv + rq, hs])
        if nv:
            acc = acc + _dot(ps[1], va_ref[0:nv, hs])
        return acc

    for c in range(nq):
        @pl.when(qi == c)
        def _(c=c):
            sample = _sample_stages(qs_ref, kn_ref, vn_ref, ln_ref, rep_ref, ut_ref, k_refs, v_refs,
                                    lf_refs, os_ref, SAMPLE_PAGE_GROUPS)
            past = c * tq
            s, p, acc = {}, {}, {}
            slots = len(units) + 2
            for t in range(max(slots, len(sample))):
                if t < len(units):
                    s[t] = scores(*units[t], past)
                if 1 <= t <= len(units):
                    p[t - 1] = probs(s.pop(t - 1))
                if 2 <= t < slots:
                    acc[t - 2] = values(p.pop(t - 2), units[t - 2][1])
                if t < len(sample):
                    sample[t]()
            for r in range(nsplit):
                a_e, a_o = acc[2 * r], acc[2 * r + 1]
                o_e = a_e / a_e[:, HEAD_DIM:HEAD_DIM + 1]
                o_o = a_o / a_o[:, 0:1]
                o_ref[r * rq:(r + 1) * rq, :] = jnp.where(low, o_e, o_o).astype(o_ref.dtype)


def _attention(layer, page_table, qa, ka, va, qs, kn, vn, ln, ck, cv, clf, rep, ut, tq):
    b, l, d2 = qa.shape
    d = d2 // 2
    nb, nh, hd = qs.shape
    npair = nh // 2
    nq = l // tq
    npg = page_table.shape[1]
    page = ck.shape[-1]
    assert nb == b * npair * nq, "one sample sequence rides along with each prompt attention step"
    pt = page_table.reshape(-1)

    seq = lambda i, p, j: (i * npair + p) * nq + j

    def page_map(n, nd):
        return lambda i, p, j, pt_ref: (layer, pt_ref[seq(i, p, j) * npg + n]) + (0,) * nd

    per_seq = lambda shape: pl.BlockSpec((None,) + shape, lambda i, p, j, pt_ref: (seq(i, p, j), 0, 0))
    const = lambda a: pl.BlockSpec(a.shape, lambda i, p, j, pt_ref: (0,) * a.ndim)
    in_specs = ([pl.BlockSpec((None, tq, 2 * LANES), lambda i, p, j, pt_ref: (i, j, p)),
                 pl.BlockSpec((None, l, 2 * LANES), lambda i, p, j, pt_ref: (i, 0, p)),
                 pl.BlockSpec((None, l, 2 * LANES), lambda i, p, j, pt_ref: (i, 0, p)),
                 per_seq((nh, hd)), per_seq((1, d)), per_seq((1, d)), per_seq((nh, 1)),
                 const(rep), const(ut)]
                + [pl.BlockSpec((None, None, nh, hd, page), page_map(n, 3)) for n in range(npg)]
                + [pl.BlockSpec((None, None, nh, hd, page), page_map(n, 3)) for n in range(npg)]
                + [pl.BlockSpec((None, None, nh, page), page_map(n, 2)) for n in range(npg)])
    grid_spec = pltpu.PrefetchScalarGridSpec(
        num_scalar_prefetch=1, grid=(b, npair, nq), in_specs=in_specs,
        out_specs=[pl.BlockSpec((None, tq, LANES), lambda i, p, j, pt_ref: (i, j, p)), per_seq((1, d))])
    return pl.pallas_call(
        functools.partial(_attention_kernel, tq=tq, nq=nq, nsplit=ATTN_ROW_PARTS, npg=npg),
        grid_spec=grid_spec,
        out_shape=[jax.ShapeDtypeStruct((b, l, d), BF16), jax.ShapeDtypeStruct((nb, 1, d), F32)],
        compiler_params=pltpu.CompilerParams(
            dimension_semantics=("arbitrary", "arbitrary", "arbitrary"),
            vmem_limit_bytes=VMEM_LIMIT),
        name="attention",
    )(pt, qa, ka, va, qs, kn, vn, ln, rep, ut, *([ck] * npg), *([cv] * npg), *([clf] * npg))


def _sample_stages(q_ref, kn_ref, vn_ref, ln_ref, rep_ref, ut_ref, k_refs, v_refs, lf_refs, o_ref,
                   ngroups):
    npg = len(k_refs)
    nh = N_HEADS
    d = nh * HEAD_DIM
    page = k_refs[0].shape[-1]
    per = -(-npg // ngroups)
    groups = [range(g * per, min((g + 1) * per, npg)) for g in range(ngroups)]
    st = {}

    def prepare():
        st['own'] = (lax.broadcasted_iota(jnp.int32, (nh, d), 1) // HEAD_DIM
                     == lax.broadcasted_iota(jnp.int32, (nh, d), 0))
        st['qrep'] = _dot(q_ref[...].astype(BF16), rep_ref[...])
        st['qbd'] = jnp.where(st['own'], st['qrep'], 0.0).astype(BF16)
        st['scores'] = [None] * npg
        pieces = []
        for p in range(npg):
            pieces.extend(_split3(lf_refs[p][...]))
        st['sfx'] = _dot(jnp.concatenate(pieces, axis=0).astype(BF16), ut_ref[...])

    def score_pages(pages):
        for p in pages:
            st['scores'][p] = _dot(st['qbd'], k_refs[p][...].reshape(d, page).astype(BF16))

    def softmax():
        own, qrep, sfx_all = st['own'], st['qrep'], st['sfx']
        car = ln_ref[...]
        z = [None] * npg
        for p in reversed(range(npg)):
            blk = sfx_all[p * N_PIECES * nh:(p + 1) * N_PIECES * nh]
            sfx = blk[0:nh] + blk[nh:2 * nh] + blk[2 * nh:3 * nh]
            z[p] = st['scores'][p] + sfx[:, 0:page] + car
            car = car + sfx[:, page:page + 1]
        s_new = jnp.sum(jnp.where(own, qrep * kn_ref[...], 0.0), axis=1, keepdims=True)
        m = s_new
        for p in range(npg):
            m = jnp.maximum(m, jnp.max(z[p], axis=1, keepdims=True))
        e_new = jnp.exp(s_new - m)
        st['e'] = [jnp.exp(z[p] - m) for p in range(npg)]
        l = e_new
        for p in range(npg):
            l = l + jnp.sum(st['e'][p], axis=1, keepdims=True)
        st['l'] = l
        st['acc'] = e_new * vn_ref[...]

    def value_pages(pages):
        for p in pages:
            st['acc'] = st['acc'] + _dot_nt(st['e'][p].astype(BF16),
                                            v_refs[p][...].reshape(d, page).astype(BF16))

    def finish():
        o_ref[...] = jnp.sum(jnp.where(st['own'], st['acc'] / st['l'], 0.0), axis=0, keepdims=True)

    stages = [prepare]
    stages += [functools.partial(score_pages, g) for g in groups]
    stages += [softmax]
    stages += [functools.partial(value_pages, g) for g in groups]
    stages += [finish]
    return stages


def _sample_constants(page):
    rep = np.tile(np.eye(HEAD_DIM, dtype=np.float32), (1, N_HEADS))
    ut = np.ones((page, 2 * page), np.float32)
    ut[:, :page] = np.arange(page)[:, None] > np.arange(page)[None, :]
    return jnp.asarray(rep, BF16), jnp.asarray(ut, BF16)


def _outproj_kernel(h_ref, yc_ref, o_ref, sgc_ref, sga_ref, p_ref, wco_ref, wao_ref, wo_ref,
                    gffn_ref, wgt_ref, wup_ref, wdn_ref, gple_ref, wpg_ref, wpp_ref, gfin_ref,
                    out_ref, *, final):
    h = h_ref[...]
    merged = (sgc_ref[...].astype(F32) * _dot(yc_ref[...], wco_ref[...])
              + sga_ref[...].astype(F32) * _dot(o_ref[...].astype(BF16), wao_ref[...]))
    h = h + _dot(merged.astype(BF16), wo_ref[...])
    hn = _rms(h, gffn_ref[...]).astype(BF16)
    gate = _dot(hn, wgt_ref[...])
    act = (gate * _sigmoid(gate) * _dot(hn, wup_ref[...])).astype(BF16)
    h = h + _dot(act, wdn_ref[...])
    hn = _rms(h, gple_ref[...]).astype(BF16)
    h = h + _sigmoid(_dot(hn, wpg_ref[...])) * _dot(p_ref[...].astype(BF16), wpp_ref[...])
    out_ref[...] = _rms(h, gfin_ref[...]) if final else h


def _outproj(h, yc, o, sgc, sga, p, lw, g_final, tm, final):
    m, d = h.shape
    dple = p.shape[1]
    row = lambda n: pl.BlockSpec((tm, n), lambda i: (i, 0))
    ws = [lw['wco'], lw['wao'], lw['wo'], lw['g_ffn'], lw['wgt'], lw['wup'], lw['wdn'],
          lw['g_ple'], lw['wpg'], lw['wpp'], g_final]
    return pl.pallas_call(
        functools.partial(_outproj_kernel, final=final),
        grid=(m // tm,),
        in_specs=[row(d), row(d), row(d), row(d), row(d), row(dple)] + [_resident(w.shape) for w in ws],
        out_specs=row(d),
        out_shape=jax.ShapeDtypeStruct((m, d), F32),
        compiler_params=pltpu.CompilerParams(dimension_semantics=("arbitrary",),
                                             vmem_limit_bytes=VMEM_LIMIT),
        name="outproj",
    )(h, yc, o, sgc, sga, p, *ws)


TM_IN = 256
TQ = 1024
ATTN_ROW_PARTS = 4
SAMPLE_PAGE_GROUPS = 3
TM_OUT = 256


def _layer_weights(i, w_in, b_forget, conv_w, w_conv_out, w_attn_out, w_o, g_mix, g_ffn, w_gate,
                   w_up, w_down, g_ple, w_ple_gate, w_ple_proj):
    d = w_in.shape[1]
    nh = N_HEADS
    wi = w_in[i]
    main = 6 * d
    return dict(
        wa=wi[:, :main].astype(BF16),
        wfl=jnp.tile(wi[:, main:main + nh], (1, N_PIECES)).astype(BF16),
        wg=wi[:, main + nh:].astype(BF16),
        bf3=jnp.tile(b_forget[i][None, :], (1, N_PIECES)),
        conv_w=conv_w[i], g_mix=g_mix[i][None, :], g_ffn=g_ffn[i][None, :], g_ple=g_ple[i][None, :],
        wco=w_conv_out[i].astype(BF16), wao=w_attn_out[i].astype(BF16), wo=w_o[i].astype(BF16),
        wgt=w_gate[i].astype(BF16), wup=w_up[i].astype(BF16), wdn=w_down[i].astype(BF16),
        wpg=w_ple_gate[i].astype(BF16), wpp=w_ple_proj[i].astype(BF16))


def kernel(x_prompt, x_sample, cache_k, cache_v, cache_logf, state_conv, page_table, p_prompt, p_sample, w_in, b_forget, conv_w, w_conv_out, w_attn_out, w_o, g_mix, g_ffn, w_gate, w_up, w_down, g_ple, w_ple_gate, w_ple_proj, g_final):
    depth = w_in.shape[0]
    b, l, d = x_prompt.shape
    nb = x_sample.shape[0]
    nh, hd = N_HEADS, HEAD_DIM
    tm_in, tq, tm_out = min(TM_IN, l), min(TQ, l), min(TM_OUT, l)
    assert x_sample.shape[1] == 1 and d == nh * hd
    assert l % tm_in == 0 and l % tq == 0 and (b * l) % tm_out == 0

    tri = jnp.tril(jnp.ones((tm_in, tm_in), F32)).astype(BF16)
    consts = _extras_constants(d) + (tri,)
    rep, ut = _sample_constants(cache_k.shape[2])
    gfin = g_final[None, :]
    ck = jnp.transpose(cache_k, (0, 1, 3, 4, 2))
    cv = jnp.transpose(cache_v, (0, 1, 3, 4, 2))
    clf = jnp.transpose(cache_logf, (0, 1, 3, 2))

    hp = x_prompt
    hs = x_sample.reshape(nb, d)
    outs = [[] for _ in range(8)]
    for i in range(depth):
        lw = _layer_weights(i, w_in, b_forget, conv_w, w_conv_out, w_attn_out, w_o, g_mix, g_ffn,
                            w_gate, w_up, w_down, g_ple, w_ple_gate, w_ple_proj)
        final = i == depth - 1

        k32, v32, lf, qa, ka, va, yc, sgc, sga, cl = _inproj_prompt(hp, lw, consts, tm_in)
        ks, vs, lfs, qs, us, ycs, sgcs, sgas = _inproj_sample(
            hs, state_conv[i, :, 0, :], state_conv[i, :, 1, :], lw)
        o, os_ = _attention(i, page_table, qa, ka, va, qs.reshape(nb, nh, hd), ks.reshape(nb, 1, d),
                            vs.reshape(nb, 1, d), lfs.reshape(nb, nh, 1), ck, cv, clf, rep, ut, tq)
        flat = lambda a: a.reshape(b * l, a.shape[-1])
        hp = _outproj(flat(hp), flat(yc), flat(o), flat(sgc), flat(sga), flat(p_prompt[i]), lw, gfin,
                      tm_out, final).reshape(b, l, d)
        hs = _outproj(hs, ycs, os_.reshape(nb, d), sgcs, sgas, p_sample[i].reshape(nb, -1), lw, gfin,
                      nb, final)

        outs[0].append(k32.reshape(b, l, nh, hd))
        outs[1].append(v32.reshape(b, l, nh, hd))
        outs[2].append(lf)
        outs[3].append(cl)
        outs[4].append(ks.reshape(nb, 1, nh, hd))
        outs[5].append(vs.reshape(nb, 1, nh, hd))
        outs[6].append(lfs.reshape(nb, 1, nh))
        outs[7].append(jnp.stack([state_conv[i, :, 1, :], us], axis=1))
    return (hp, hs.reshape(nb, 1, d)) + tuple(jnp.stack(o_) for o_ in outs)
```

```python
import functools

import numpy as np
import jax
import jax.numpy as jnp
from jax import lax
from jax.experimental import pallas as pl
from jax.experimental.pallas import tpu as pltpu

N_HEADS = 16
HEAD_DIM = 64
CONV_WIDTH = 3
RMS_EPS = 1e-6
LANES = 128
SUBLANES = 8
LOG2E = 1.4426950408889634
N_PIECES = 3
VMEM_LIMIT = 56 * 1024 * 1024

F32 = jnp.float32
BF16 = jnp.bfloat16


def _dot(a, b):
    return jnp.dot(a, b, preferred_element_type=F32)


def _dot_nt(a, b):
    return lax.dot_general(a, b, (((1,), (1,)), ((), ())), preferred_element_type=F32)


def _rms(x, g):
    return x * lax.rsqrt(jnp.mean(x * x, axis=-1, keepdims=True) + RMS_EPS) * g


def _sigmoid(x):
    return 1.0 / (1.0 + jnp.exp(-x))


def _log_sigmoid(x):
    return jnp.minimum(x, 0.0) - jnp.log1p(jnp.exp(-jnp.abs(x)))


def _split3(x):
    p1 = x.astype(BF16).astype(F32)
    r = x - p1
    p2 = r.astype(BF16).astype(F32)
    p3 = (r - p2).astype(BF16).astype(F32)
    return p1, p2, p3


def _resident(shape):
    nd = len(shape)
    return pl.BlockSpec(shape, lambda *_: (0,) * nd, pipeline_mode=pl.Buffered(1))


def _inproj_prompt_kernel(h_ref, g_ref, wa_ref, wfl_ref, wg_ref, bf_ref, cw_ref, pq_ref, pk_ref,
                          oq_ref, ok_ref, tri_ref, *refs, tm, d, first):
    (kt_ref, vt_ref, lf_ref, qa_ref, ka_ref, va_ref, yc_ref, sgc_ref, sga_ref, cl_ref,
     ubuf, ccar) = refs if first else refs[2:]
    nh = N_HEADS

    @pl.when(pl.program_id(1) == 0)
    def _():
        ubuf[0:SUBLANES, :] = jnp.zeros((SUBLANES, d), F32)
        ccar[...] = jnp.zeros_like(ccar)

    hn = _rms(h_ref[...], g_ref[...]).astype(BF16)

    bg = _dot(hn, wa_ref[:, 0:d])
    u = _dot(hn, wa_ref[:, d:2 * d]) * _dot(hn, wa_ref[:, 2 * d:3 * d])
    ubuf[SUBLANES:SUBLANES + tm, :] = u
    um1 = ubuf[SUBLANES - 1:SUBLANES - 1 + tm, :]
    um2 = ubuf[SUBLANES - 2:SUBLANES - 2 + tm, :]
    conv = cw_ref[0:1, :] * um2 + cw_ref[1:2, :] * um1 + cw_ref[2:3, :] * u
    yc_ref[...] = (bg * conv).astype(BF16)
    ubuf[0:SUBLANES, :] = u[tm - SUBLANES:tm, :]
    cl_ref[...] = u[tm - (CONV_WIDTH - 1):tm, :]

    lf3 = _log_sigmoid(_dot(hn, wfl_ref[...]) + bf_ref[...])
    lf_ref[...] = lf3[:, 0:nh]
    l1, l2, l3 = _split3(lf3)
    tri = tri_ref[...]
    cum = (_dot(tri, l1.astype(BF16)) + _dot(tri, l2.astype(BF16)) + _dot(tri, l3.astype(BF16))
           + ccar[...])
    ccar[...] = cum[tm - 1:tm, :]
    c1, c2, c3 = _split3(cum * LOG2E)
    lane3 = lax.broadcasted_iota(jnp.int32, cum.shape, 1)
    cpieces = jnp.where(lane3 < nh, c1, jnp.where(lane3 < 2 * nh, c2, c3)).astype(BF16)
    xq = _dot(cpieces, pq_ref[...]) + oq_ref[...]
    xk = _dot(cpieces, pk_ref[...]) + ok_ref[...]

    q = _dot(hn, wa_ref[:, 3 * d:4 * d]) * (HEAD_DIM ** -0.5 * LOG2E)
    k = _dot(hn, wa_ref[:, 4 * d:5 * d])
    kt_ref[...] = k.T
    v = _dot(hn, wa_ref[:, 5 * d:6 * d])
    vt_ref[...] = v.T
    low = lax.broadcasted_iota(jnp.int32, (tm, LANES), 1) < HEAD_DIM
    for p in range(nh // 2):
        sl = slice(p * LANES, (p + 1) * LANES)
        ev = slice(2 * p * LANES, (2 * p + 1) * LANES)
        od = slice((2 * p + 1) * LANES, (2 * p + 2) * LANES)
        qa_ref[:, ev] = jnp.where(low, q[:, sl], xq[:, sl]).astype(BF16)
        qa_ref[:, od] = jnp.where(low, xq[:, sl], q[:, sl]).astype(BF16)
        ka_ref[:, ev] = jnp.where(low, k[:, sl], xk[:, sl]).astype(BF16)
        ka_ref[:, od] = jnp.where(low, xk[:, sl], k[:, sl]).astype(BF16)
        va_ref[:, ev] = jnp.where(low, v[:, sl], 1.0).astype(BF16)
        va_ref[:, od] = jnp.where(low, 1.0, v[:, sl]).astype(BF16)

    sgc_ref[...] = _sigmoid(_dot(hn, wg_ref[:, 0:d])).astype(BF16)
    sga_ref[...] = _sigmoid(_dot(hn, wg_ref[:, d:2 * d])).astype(BF16)


def _inproj_sample_kernel(h_ref, g_ref, wa_ref, wfl_ref, wg_ref, bf_ref, cw_ref, um2_ref, um1_ref,
                          k32_ref, v32_ref, lf_ref, q_ref, u_ref, yc_ref, sgc_ref, sga_ref, *, d):
    nh = N_HEADS
    hn = _rms(h_ref[...], g_ref[...]).astype(BF16)
    bg = _dot(hn, wa_ref[:, 0:d])
    u = _dot(hn, wa_ref[:, d:2 * d]) * _dot(hn, wa_ref[:, 2 * d:3 * d])
    conv = cw_ref[0:1, :] * um2_ref[...] + cw_ref[1:2, :] * um1_ref[...] + cw_ref[2:3, :] * u
    yc_ref[...] = (bg * conv).astype(BF16)
    u_ref[...] = u
    lf3 = _log_sigmoid(_dot(hn, wfl_ref[...]) + bf_ref[...])
    lf_ref[...] = lf3[:, 0:nh]
    q_ref[...] = _dot(hn, wa_ref[:, 3 * d:4 * d]) * (HEAD_DIM ** -0.5)
    k32_ref[...] = _dot(hn, wa_ref[:, 4 * d:5 * d])
    v32_ref[...] = _dot(hn, wa_ref[:, 5 * d:6 * d])
    sgc_ref[...] = _sigmoid(_dot(hn, wg_ref[:, 0:d])).astype(BF16)
    sga_ref[...] = _sigmoid(_dot(hn, wg_ref[:, d:2 * d])).astype(BF16)


def _extras_constants(d):
    nh = N_HEADS
    pq = np.zeros((N_PIECES * nh, d), np.float32)
    pk = np.zeros((N_PIECES * nh, d), np.float32)
    oq = np.zeros((1, d), np.float32)
    ok = np.zeros((1, d), np.float32)
    for h in range(nh):
        base = (h // 2) * LANES + (HEAD_DIM if h % 2 == 0 else 0)
        for j in range(N_PIECES):
            pq[j * nh + h, base + j] = 1.0
            pk[j * nh + h, base + N_PIECES + j] = -1.0
            oq[0, base + N_PIECES + j] = 1.0
            ok[0, base + j] = 1.0
    return (jnp.asarray(pq, BF16), jnp.asarray(pk, BF16), jnp.asarray(oq), jnp.asarray(ok))


def _inproj_prompt(h, lw, consts, tm, layer, depth, kt_all, vt_all):
    b, l, d = h.shape
    nh = N_HEADS
    pq, pk, oq, ok, tri = consts
    first = kt_all is None
    grid = (b, l // tm)
    row = lambda n: pl.BlockSpec((None, tm, n), lambda i, j: (i, j, 0))
    in_specs = [row(d), _resident((1, d)), _resident(lw['wa'].shape), _resident(lw['wfl'].shape),
                _resident(lw['wg'].shape), _resident((1, N_PIECES * nh)),
                _resident((CONV_WIDTH, d)), _resident(pq.shape), _resident(pk.shape),
                _resident(oq.shape), _resident(ok.shape), _resident(tri.shape)]
    operands = [h, lw['g_mix'], lw['wa'], lw['wfl'], lw['wg'], lw['bf3'], lw['conv_w'], pq, pk, oq, ok,
                tri]
    aliases = {}
    if not first:
        aliases = {len(operands): 0, len(operands) + 1: 1}
        in_specs += [pl.BlockSpec(memory_space=pl.ANY)] * 2
        operands += [kt_all, vt_all]
    act = lambda n, dt: jax.ShapeDtypeStruct((b, l, n), dt)
    stacked = jax.ShapeDtypeStruct((depth, b, d, l), F32)
    slab = pl.BlockSpec((None, None, d, tm), lambda i, j: (layer, i, 0, j))
    out_shape = [stacked, stacked, act(nh, F32), act(2 * d, BF16), act(2 * d, BF16),
                 act(2 * d, BF16), act(d, BF16), act(d, BF16), act(d, BF16),
                 jax.ShapeDtypeStruct((b, CONV_WIDTH - 1, d), F32)]
    out_specs = [slab, slab, row(nh), row(2 * d), row(2 * d), row(2 * d), row(d), row(d), row(d),
                 pl.BlockSpec((None, CONV_WIDTH - 1, d), lambda i, j: (i, 0, 0))]
    return pl.pallas_call(
        functools.partial(_inproj_prompt_kernel, tm=tm, d=d, first=first),
        grid=grid, in_specs=in_specs, out_specs=out_specs, out_shape=out_shape,
        input_output_aliases=aliases,
        scratch_shapes=[pltpu.VMEM((tm + SUBLANES, d), F32), pltpu.VMEM((1, N_PIECES * nh), F32)],
        compiler_params=pltpu.CompilerParams(dimension_semantics=("arbitrary", "arbitrary"),
                                             vmem_limit_bytes=VMEM_LIMIT),
        name="inproj_prompt",
    )(*operands)


def _inproj_sample(h, um2, um1, lw):
    m, d = h.shape
    nh = N_HEADS
    full = lambda n, dt: jax.ShapeDtypeStruct((m, n), dt)
    out_shape = [full(d, F32), full(d, F32), full(nh, F32), full(d, F32), full(d, F32),
                 full(d, BF16), full(d, BF16), full(d, BF16)]
    return pl.pallas_call(
        functools.partial(_inproj_sample_kernel, d=d),
        out_shape=out_shape,
        compiler_params=pltpu.CompilerParams(vmem_limit_bytes=VMEM_LIMIT),
        name="inproj_sample",
    )(h, lw['g_mix'], lw['wa'], lw['wfl'], lw['wg'], lw['bf3'], lw['conv_w'], um2, um1)


def _attention_kernel(pt_ref, qa_ref, ka_ref, va_ref, qs_ref, kn_ref, vn_ref, ln_ref, rep_ref, ut_ref,
                      *refs, tq, nq, nsplit, npg):
    del pt_ref
    k_refs, v_refs, lf_refs = refs[0:npg], refs[npg:2 * npg], refs[2 * npg:3 * npg]
    o_ref, os_ref = refs[3 * npg:]
    qi = pl.program_id(2)
    rq = tq // nsplit
    low = lax.broadcasted_iota(jnp.int32, (rq, LANES), 1) < HEAD_DIM
    units = [(r, slice(par * LANES, (par + 1) * LANES)) for r in range(nsplit) for par in range(2)]

    causal = (lax.broadcasted_iota(jnp.int32, (rq, rq), 1)
              <= lax.broadcasted_iota(jnp.int32, (rq, rq), 0))

    def scores(r, hs, past):
        nv = past + r * rq
        q = qa_ref[r * rq:(r + 1) * rq, hs]
        out = [jnp.where(causal, _dot_nt(q, ka_ref[nv:nv + rq, hs]), -jnp.inf)]
        if nv:
            out.append(_dot_nt(q, ka_ref[0:nv, hs]))
        return out

    def probs(ss):
        m = jnp.max(ss[0], axis=1, keepdims=True)
        for s in ss[1:]:
            m = jnp.maximum(m, jnp.max(s, axis=1, keepdims=True))
        return [jnp.exp2(s - m).astype(BF16) for s in ss]

    def values(ps, hs):
        nv = ps[1].shape[1] if len(ps) > 1 else 0
        acc = _dot(ps[0], va_ref[nv:nv + rq, hs])
        if nv:
            acc = acc + _dot(ps[1], va_ref[0:nv, hs])
        return acc

    for c in range(nq):
        @pl.when(qi == c)
        def _(c=c):
            sample = _sample_stages(qs_ref, kn_ref, vn_ref, ln_ref, rep_ref, ut_ref, k_refs, v_refs,
                                    lf_refs, os_ref, SAMPLE_PAGE_GROUPS)
            past = c * tq
            s, p, acc = {}, {}, {}
            slots = len(units) + 2
            for t in range(max(slots, len(sample))):
                if t < len(units):
                    s[t] = scores(*units[t], past)
                if 1 <= t <= len(units):
                    p[t - 1] = probs(s.pop(t - 1))
                if 2 <= t < slots:
                    acc[t - 2] = values(p.pop(t - 2), units[t - 2][1])
                if t < len(sample):
                    sample[t]()
            for r in range(nsplit):
                a_e, a_o = acc[2 * r], acc[2 * r + 1]
                o_e = a_e / a_e[:, HEAD_DIM:HEAD_DIM + 1]
                o_o = a_o / a_o[:, 0:1]
                o_ref[r * rq:(r + 1) * rq, :] = jnp.where(low, o_e, o_o).astype(o_ref.dtype)


def _attention(layer, page_table, qa, ka, va, qs, kn, vn, ln, ck, cv, clf, rep, ut, tq):
    b, l, d2 = qa.shape
    d = d2 // 2
    nb, nh, hd = qs.shape
    npair = nh // 2
    nq = l // tq
    npg = page_table.shape[1]
    page = ck.shape[-1]
    assert nb == b * npair * nq, "one sample sequence rides along with each prompt attention step"
    pt = page_table.reshape(-1)

    seq = lambda i, p, j: (i * npair + p) * nq + j

    def page_map(n, nd):
        return lambda i, p, j, pt_ref: (layer, pt_ref[seq(i, p, j) * npg + n]) + (0,) * nd

    per_seq = lambda shape: pl.BlockSpec((None,) + shape, lambda i, p, j, pt_ref: (seq(i, p, j), 0, 0))
    const = lambda a: pl.BlockSpec(a.shape, lambda i, p, j, pt_ref: (0,) * a.ndim)
    in_specs = ([pl.BlockSpec((None, tq, 2 * LANES), lambda i, p, j, pt_ref: (i, j, p)),
                 pl.BlockSpec((None, l, 2 * LANES), lambda i, p, j, pt_ref: (i, 0, p)),
                 pl.BlockSpec((None, l, 2 * LANES), lambda i, p, j, pt_ref: (i, 0, p)),
                 per_seq((nh, hd)), per_seq((1, d)), per_seq((1, d)), per_seq((nh, 1)),
                 const(rep), const(ut)]
                + [pl.BlockSpec((None, None, nh, hd, page), page_map(n, 3)) for n in range(npg)]
                + [pl.BlockSpec((None, None, nh, hd, page), page_map(n, 3)) for n in range(npg)]
                + [pl.BlockSpec((None, None, nh, page), page_map(n, 2)) for n in range(npg)])
    grid_spec = pltpu.PrefetchScalarGridSpec(
        num_scalar_prefetch=1, grid=(b, npair, nq), in_specs=in_specs,
        out_specs=[pl.BlockSpec((None, tq, LANES), lambda i, p, j, pt_ref: (i, j, p)), per_seq((1, d))])
    return pl.pallas_call(
        functools.partial(_attention_kernel, tq=tq, nq=nq, nsplit=ATTN_ROW_PARTS, npg=npg),
        grid_spec=grid_spec,
        out_shape=[jax.ShapeDtypeStruct((b, l, d), BF16), jax.ShapeDtypeStruct((nb, 1, d), F32)],
        compiler_params=pltpu.CompilerParams(
            dimension_semantics=("arbitrary", "arbitrary", "arbitrary"),
            vmem_limit_bytes=VMEM_LIMIT),
        name="attention",
    )(pt, qa, ka, va, qs, kn, vn, ln, rep, ut, *([ck] * npg), *([cv] * npg), *([clf] * npg))


def _sample_stages(q_ref, kn_ref, vn_ref, ln_ref, rep_ref, ut_ref, k_refs, v_refs, lf_refs, o_ref,
                   ngroups):
    npg = len(k_refs)
    nh = N_HEADS
    d = nh * HEAD_DIM
    page = k_refs[0].shape[-1]
    per = -(-npg // ngroups)
    groups = [range(g * per, min((g + 1) * per, npg)) for g in range(ngroups)]
    st = {}

    def prepare():
        st['own'] = (lax.broadcasted_iota(jnp.int32, (nh, d), 1) // HEAD_DIM
                     == lax.broadcasted_iota(jnp.int32, (nh, d), 0))
        st['qrep'] = _dot(q_ref[...].astype(BF16), rep_ref[...])
        st['qbd'] = jnp.where(st['own'], st['qrep'], 0.0).astype(BF16)
        st['scores'] = [None] * npg
        pieces = []
        for p in range(npg):
            pieces.extend(_split3(lf_refs[p][...]))
        st['sfx'] = _dot(jnp.concatenate(pieces, axis=0).astype(BF16), ut_ref[...])

    def score_pages(pages):
        for p in pages:
            st['scores'][p] = _dot(st['qbd'], k_refs[p][...].reshape(d, page).astype(BF16))

    def softmax():
        own, qrep, sfx_all = st['own'], st['qrep'], st['sfx']
        car = ln_ref[...]
        z = [None] * npg
        for p in reversed(range(npg)):
            blk = sfx_all[p * N_PIECES * nh:(p + 1) * N_PIECES * nh]
            sfx = blk[0:nh] + blk[nh:2 * nh] + blk[2 * nh:3 * nh]
            z[p] = st['scores'][p] + sfx[:, 0:page] + car
            car = car + sfx[:, page:page + 1]
        s_new = jnp.sum(jnp.where(own, qrep * kn_ref[...], 0.0), axis=1, keepdims=True)
        m = s_new
        for p in range(npg):
            m = jnp.maximum(m, jnp.max(z[p], axis=1, keepdims=True))
        e_new = jnp.exp(s_new - m)
        st['e'] = [jnp.exp(z[p] - m) for p in range(npg)]
        l = e_new
        for p in range(npg):
            l = l + jnp.sum(st['e'][p], axis=1, keepdims=True)
        st['l'] = l
        st['acc'] = e_new * vn_ref[...]

    def value_pages(pages):
        for p in pages:
            st['acc'] = st['acc'] + _dot_nt(st['e'][p].astype(BF16),
                                            v_refs[p][...].reshape(d, page).astype(BF16))

    def finish():
        o_ref[...] = jnp.sum(jnp.where(st['own'], st['acc'] / st['l'], 0.0), axis=0, keepdims=True)

    stages = [prepare]
    stages += [functools.partial(score_pages, g) for g in groups]
    stages += [softmax]
    stages += [functools.partial(value_pages, g) for g in groups]
    stages += [finish]
    return stages


def _sample_constants(page):
    rep = np.tile(np.eye(HEAD_DIM, dtype=np.float32), (1, N_HEADS))
    ut = np.ones((page, 2 * page), np.float32)
    ut[:, :page] = np.arange(page)[:, None] > np.arange(page)[None, :]
    return jnp.asarray(rep, BF16), jnp.asarray(ut, BF16)


def _outproj_kernel(h_ref, yc_ref, o_ref, sgc_ref, sga_ref, p_ref, wco_ref, wao_ref, wo_ref,
                    gffn_ref, wgt_ref, wup_ref, wdn_ref, gple_ref, wpg_ref, wpp_ref, gfin_ref,
                    out_ref, *, final):
    h = h_ref[...]
    merged = (sgc_ref[...].astype(F32) * _dot(yc_ref[...], wco_ref[...])
              + sga_ref[...].astype(F32) * _dot(o_ref[...].astype(BF16), wao_ref[...]))
    h = h + _dot(merged.astype(BF16), wo_ref[...])
    hn = _rms(h, gffn_ref[...]).astype(BF16)
    gate = _dot(hn, wgt_ref[...])
    act = (gate * _sigmoid(gate) * _dot(hn, wup_ref[...])).astype(BF16)
    h = h + _dot(act, wdn_ref[...])
    hn = _rms(h, gple_ref[...]).astype(BF16)
    h = h + _sigmoid(_dot(hn, wpg_ref[...])) * _dot(p_ref[...].astype(BF16), wpp_ref[...])
    out_ref[...] = _rms(h, gfin_ref[...]) if final else h


def _outproj(h, yc, o, sgc, sga, p, lw, g_final, tm, final):
    m, d = h.shape
    dple = p.shape[1]
    row = lambda n: pl.BlockSpec((tm, n), lambda i: (i, 0))
    ws = [lw['wco'], lw['wao'], lw['wo'], lw['g_ffn'], lw['wgt'], lw['wup'], lw['wdn'],
          lw['g_ple'], lw['wpg'], lw['wpp'], g_final]
    return pl.pallas_call(
        functools.partial(_outproj_kernel, final=final),
        grid=(m // tm,),
        in_specs=[row(d), row(d), row(d), row(d), row(d), row(dple)] + [_resident(w.shape) for w in ws],
        out_specs=row(d),
        out_shape=jax.ShapeDtypeStruct((m, d), F32),
        compiler_params=pltpu.CompilerParams(dimension_semantics=("arbitrary",),
                                             vmem_limit_bytes=VMEM_LIMIT),
        name="outproj",
    )(h, yc, o, sgc, sga, p, *ws)


TM_IN = 256
TQ = 1024
ATTN_ROW_PARTS = 4
SAMPLE_PAGE_GROUPS = 3
TM_OUT = 256


def _layer_weights(i, w_in, b_forget, conv_w, w_conv_out, w_attn_out, w_o, g_mix, g_ffn, w_gate,
                   w_up, w_down, g_ple, w_ple_gate, w_ple_proj):
    d = w_in.shape[1]
    nh = N_HEADS
    wi = w_in[i]
    main = 6 * d
    return dict(
        wa=wi[:, :main].astype(BF16),
        wfl=jnp.tile(wi[:, main:main + nh], (1, N_PIECES)).astype(BF16),
        wg=wi[:, main + nh:].astype(BF16),
        bf3=jnp.tile(b_forget[i][None, :], (1, N_PIECES)),
        conv_w=conv_w[i], g_mix=g_mix[i][None, :], g_ffn=g_ffn[i][None, :], g_ple=g_ple[i][None, :],
        wco=w_conv_out[i].astype(BF16), wao=w_attn_out[i].astype(BF16), wo=w_o[i].astype(BF16),
        wgt=w_gate[i].astype(BF16), wup=w_up[i].astype(BF16), wdn=w_down[i].astype(BF16),
        wpg=w_ple_gate[i].astype(BF16), wpp=w_ple_proj[i].astype(BF16))


def kernel(x_prompt, x_sample, cache_k, cache_v, cache_logf, state_conv, page_table, p_prompt, p_sample, w_in, b_forget, conv_w, w_conv_out, w_attn_out, w_o, g_mix, g_ffn, w_gate, w_up, w_down, g_ple, w_ple_gate, w_ple_proj, g_final):
    depth = w_in.shape[0]
    b, l, d = x_prompt.shape
    nb = x_sample.shape[0]
    nh, hd = N_HEADS, HEAD_DIM
    tm_in, tq, tm_out = min(TM_IN, l), min(TQ, l), min(TM_OUT, l)
    assert x_sample.shape[1] == 1 and d == nh * hd
    assert l % tm_in == 0 and l % tq == 0 and (b * l) % tm_out == 0

    tri = jnp.tril(jnp.ones((tm_in, tm_in), F32)).astype(BF16)
    consts = _extras_constants(d) + (tri,)
    rep, ut = _sample_constants(cache_k.shape[2])
    gfin = g_final[None, :]
    ck = jnp.transpose(cache_k, (0, 1, 3, 4, 2))
    cv = jnp.transpose(cache_v, (0, 1, 3, 4, 2))
    clf = jnp.transpose(cache_logf, (0, 1, 3, 2))

    hp = x_prompt
    hs = x_sample.reshape(nb, d)
    outs = [[] for _ in range(6)]
    kt_all = vt_all = None
    for i in range(depth):
        lw = _layer_weights(i, w_in, b_forget, conv_w, w_conv_out, w_attn_out, w_o, g_mix, g_ffn,
                            w_gate, w_up, w_down, g_ple, w_ple_gate, w_ple_proj)
        final = i == depth - 1

        kt_all, vt_all, lf, qa, ka, va, yc, sgc, sga, cl = _inproj_prompt(
            hp, lw, consts, tm_in, i, depth, kt_all, vt_all)
        ks, vs, lfs, qs, us, ycs, sgcs, sgas = _inproj_sample(
            hs, state_conv[i, :, 0, :], state_conv[i, :, 1, :], lw)
        o, os_ = _attention(i, page_table, qa, ka, va, qs.reshape(nb, nh, hd), ks.reshape(nb, 1, d),
                            vs.reshape(nb, 1, d), lfs.reshape(nb, nh, 1), ck, cv, clf, rep, ut, tq)
        flat = lambda a: a.reshape(b * l, a.shape[-1])
        hp = _outproj(flat(hp), flat(yc), flat(o), flat(sgc), flat(sga), flat(p_prompt[i]), lw, gfin,
                      tm_out, final).reshape(b, l, d)
        hs = _outproj(hs, ycs, os_.reshape(nb, d), sgcs, sgas, p_sample[i].reshape(nb, -1), lw, gfin,
                      nb, final)

        outs[0].append(lf)
        outs[1].append(cl)
        outs[2].append(ks.reshape(nb, 1, nh, hd))
        outs[3].append(vs.reshape(nb, 1, nh, hd))
        outs[4].append(lfs.reshape(nb, 1, nh))
        outs[5].append(jnp.stack([state_conv[i, :, 1, :], us], axis=1))
    per_token = lambda t: jnp.transpose(t.reshape(depth, b, nh, hd, l), (0, 1, 4, 2, 3))
    lf_p, cl_p, k_s, v_s, lf_s, cl_s = (jnp.stack(o_) for o_ in outs)
    return (hp, hs.reshape(nb, 1, d), per_token(kt_all), per_token(vt_all), lf_p, cl_p, k_s, v_s, lf_s, cl_s)
```

```python
import functools

import numpy as np
import jax
import jax.numpy as jnp
from jax import lax
from jax.experimental import pallas as pl
from jax.experimental.pallas import tpu as pltpu

N_HEADS = 16
HEAD_DIM = 64
CONV_WIDTH = 3
RMS_EPS = 1e-6
LANES = 128
SUBLANES = 8
LOG2E = 1.4426950408889634
N_PIECES = 3
VMEM_LIMIT = 56 * 1024 * 1024

F32 = jnp.float32
BF16 = jnp.bfloat16


def _dot(a, b):
    return jnp.dot(a, b, preferred_element_type=F32)


def _dot_nt(a, b):
    return lax.dot_general(a, b, (((1,), (1,)), ((), ())), preferred_element_type=F32)


def _rms(x, g):
    return x * lax.rsqrt(jnp.mean(x * x, axis=-1, keepdims=True) + RMS_EPS) * g


def _sigmoid(x):
    return 1.0 / (1.0 + jnp.exp(-x))


def _log_sigmoid(x):
    return jnp.minimum(x, 0.0) - jnp.log1p(jnp.exp(-jnp.abs(x)))


def _split3(x):
    p1 = x.astype(BF16).astype(F32)
    r = x - p1
    p2 = r.astype(BF16).astype(F32)
    p3 = (r - p2).astype(BF16).astype(F32)
    return p1, p2, p3


def _resident(shape):
    nd = len(shape)
    return pl.BlockSpec(shape, lambda *_: (0,) * nd, pipeline_mode=pl.Buffered(1))


def _inproj_prompt_kernel(h_ref, g_ref, wa_ref, wfl_ref, wg_ref, bf_ref, cw_ref, pq_ref, pk_ref,
                          oq_ref, ok_ref, tri_ref, kt_in_ref, vt_in_ref,
                          kt_ref, vt_ref, lf_ref, qa_ref, ka_ref, va_ref, yc_ref, sgc_ref, sga_ref,
                          cl_ref, ubuf, ccar, *, tm, d):
    del kt_in_ref, vt_in_ref
    nh = N_HEADS

    @pl.when(pl.program_id(1) == 0)
    def _():
        ubuf[0:SUBLANES, :] = jnp.zeros((SUBLANES, d), F32)
        ccar[...] = jnp.zeros_like(ccar)

    hn = _rms(h_ref[...], g_ref[...]).astype(BF16)

    bg = _dot(hn, wa_ref[:, 0:d])
    u = _dot(hn, wa_ref[:, d:2 * d]) * _dot(hn, wa_ref[:, 2 * d:3 * d])
    ubuf[SUBLANES:SUBLANES + tm, :] = u
    um1 = ubuf[SUBLANES - 1:SUBLANES - 1 + tm, :]
    um2 = ubuf[SUBLANES - 2:SUBLANES - 2 + tm, :]
    conv = cw_ref[0:1, :] * um2 + cw_ref[1:2, :] * um1 + cw_ref[2:3, :] * u
    yc_ref[...] = (bg * conv).astype(BF16)
    ubuf[0:SUBLANES, :] = u[tm - SUBLANES:tm, :]
    cl_ref[...] = u[tm - (CONV_WIDTH - 1):tm, :]

    lf3 = _log_sigmoid(_dot(hn, wfl_ref[...]) + bf_ref[...])
    lf_ref[...] = lf3[:, 0:nh]
    l1, l2, l3 = _split3(lf3)
    tri = tri_ref[...]
    cum = (_dot(tri, l1.astype(BF16)) + _dot(tri, l2.astype(BF16)) + _dot(tri, l3.astype(BF16))
           + ccar[...])
    ccar[...] = cum[tm - 1:tm, :]
    c1, c2, c3 = _split3(cum * LOG2E)
    lane3 = lax.broadcasted_iota(jnp.int32, cum.shape, 1)
    cpieces = jnp.where(lane3 < nh, c1, jnp.where(lane3 < 2 * nh, c2, c3)).astype(BF16)
    xq = _dot(cpieces, pq_ref[...]) + oq_ref[...]
    xk = _dot(cpieces, pk_ref[...]) + ok_ref[...]

    q = _dot(hn, wa_ref[:, 3 * d:4 * d]) * (HEAD_DIM ** -0.5 * LOG2E)
    k = _dot(hn, wa_ref[:, 4 * d:5 * d])
    kt_ref[...] = k.T
    v = _dot(hn, wa_ref[:, 5 * d:6 * d])
    vt_ref[...] = v.T
    low = lax.broadcasted_iota(jnp.int32, (tm, LANES), 1) < HEAD_DIM
    for p in range(nh // 2):
        sl = slice(p * LANES, (p + 1) * LANES)
        ev = slice(2 * p * LANES, (2 * p + 1) * LANES)
        od = slice((2 * p + 1) * LANES, (2 * p + 2) * LANES)
        qa_ref[:, ev] = jnp.where(low, q[:, sl], xq[:, sl]).astype(BF16)
        qa_ref[:, od] = jnp.where(low, xq[:, sl], q[:, sl]).astype(BF16)
        ka_ref[:, ev] = jnp.where(low, k[:, sl], xk[:, sl]).astype(BF16)
        ka_ref[:, od] = jnp.where(low, xk[:, sl], k[:, sl]).astype(BF16)
        va_ref[:, ev] = jnp.where(low, v[:, sl], 1.0).astype(BF16)
        va_ref[:, od] = jnp.where(low, 1.0, v[:, sl]).astype(BF16)

    sgc_ref[...] = _sigmoid(_dot(hn, wg_ref[:, 0:d])).astype(BF16)
    sga_ref[...] = _sigmoid(_dot(hn, wg_ref[:, d:2 * d])).astype(BF16)


def _inproj_sample_kernel(h_ref, g_ref, wa_ref, wfl_ref, wg_ref, bf_ref, cw_ref, um2_ref, um1_ref,
                          k32_ref, v32_ref, lf_ref, q_ref, u_ref, yc_ref, sgc_ref, sga_ref, *, d):
    nh = N_HEADS
    hn = _rms(h_ref[...], g_ref[...]).astype(BF16)
    bg = _dot(hn, wa_ref[:, 0:d])
    u = _dot(hn, wa_ref[:, d:2 * d]) * _dot(hn, wa_ref[:, 2 * d:3 * d])
    conv = cw_ref[0:1, :] * um2_ref[...] + cw_ref[1:2, :] * um1_ref[...] + cw_ref[2:3, :] * u
    yc_ref[...] = (bg * conv).astype(BF16)
    u_ref[...] = u
    lf3 = _log_sigmoid(_dot(hn, wfl_ref[...]) + bf_ref[...])
    lf_ref[...] = lf3[:, 0:nh]
    q_ref[...] = _dot(hn, wa_ref[:, 3 * d:4 * d]) * (HEAD_DIM ** -0.5)
    k32_ref[...] = _dot(hn, wa_ref[:, 4 * d:5 * d])
    v32_ref[...] = _dot(hn, wa_ref[:, 5 * d:6 * d])
    sgc_ref[...] = _sigmoid(_dot(hn, wg_ref[:, 0:d])).astype(BF16)
    sga_ref[...] = _sigmoid(_dot(hn, wg_ref[:, d:2 * d])).astype(BF16)


def _extras_constants(d):
    nh = N_HEADS
    pq = np.zeros((N_PIECES * nh, d), np.float32)
    pk = np.zeros((N_PIECES * nh, d), np.float32)
    oq = np.zeros((1, d), np.float32)
    ok = np.zeros((1, d), np.float32)
    for h in range(nh):
        base = (h // 2) * LANES + (HEAD_DIM if h % 2 == 0 else 0)
        for j in range(N_PIECES):
            pq[j * nh + h, base + j] = 1.0
            pk[j * nh + h, base + N_PIECES + j] = -1.0
            oq[0, base + N_PIECES + j] = 1.0
            ok[0, base + j] = 1.0
    return (jnp.asarray(pq, BF16), jnp.asarray(pk, BF16), jnp.asarray(oq), jnp.asarray(ok))


def _inproj_prompt(h, lw, consts, tm, layer, depth, kt_all, vt_all):
    b, l, d = h.shape
    nh = N_HEADS
    pq, pk, oq, ok, tri = consts
    grid = (b, l // tm)
    row = lambda n: pl.BlockSpec((None, tm, n), lambda i, j: (i, j, 0))
    in_specs = [row(d), _resident((1, d)), _resident(lw['wa'].shape), _resident(lw['wfl'].shape),
                _resident(lw['wg'].shape), _resident((1, N_PIECES * nh)),
                _resident((CONV_WIDTH, d)), _resident(pq.shape), _resident(pk.shape),
                _resident(oq.shape), _resident(ok.shape), _resident(tri.shape)]
    operands = [h, lw['g_mix'], lw['wa'], lw['wfl'], lw['wg'], lw['bf3'], lw['conv_w'], pq, pk, oq, ok,
                tri]
    aliases = {len(operands): 0, len(operands) + 1: 1}
    in_specs += [pl.BlockSpec(memory_space=pl.ANY)] * 2
    operands += [kt_all, vt_all]
    act =lambda n, dt: jax.ShapeDtypeStruct((b, l, n), dt)
    stacked = jax.ShapeDtypeStruct((depth, b, d, l), F32)
    slab = pl.BlockSpec((None, None, d, tm), lambda i, j: (layer, i, 0, j))
    out_shape = [stacked, stacked, act(nh, F32), act(2 * d, BF16), act(2 * d, BF16),
                 act(2 * d, BF16), act(d, BF16), act(d, BF16), act(d, BF16),
                 jax.ShapeDtypeStruct((b, CONV_WIDTH - 1, d), F32)]
    out_specs = [slab, slab, row(nh), row(2 * d), row(2 * d), row(2 * d), row(d), row(d), row(d),
                 pl.BlockSpec((None, CONV_WIDTH - 1, d), lambda i, j: (i, 0, 0))]
    return pl.pallas_call(
        functools.partial(_inproj_prompt_kernel, tm=tm, d=d),
        grid=grid, in_specs=in_specs, out_specs=out_specs, out_shape=out_shape,
        input_output_aliases=aliases,
        scratch_shapes=[pltpu.VMEM((tm + SUBLANES, d), F32), pltpu.VMEM((1, N_PIECES * nh), F32)],
        compiler_params=pltpu.CompilerParams(dimension_semantics=("arbitrary", "arbitrary"),
                                             vmem_limit_bytes=VMEM_LIMIT),
        name="inproj_prompt",
    )(*operands)


def _inproj_sample(h, um2, um1, lw):
    m, d = h.shape
    nh = N_HEADS
    full = lambda n, dt: jax.ShapeDtypeStruct((m, n), dt)
    out_shape = [full(d, F32), full(d, F32), full(nh, F32), full(d, F32), full(d, F32),
                 full(d, BF16), full(d, BF16), full(d, BF16)]
    return pl.pallas_call(
        functools.partial(_inproj_sample_kernel, d=d),
        out_shape=out_shape,
        compiler_params=pltpu.CompilerParams(vmem_limit_bytes=VMEM_LIMIT),
        name="inproj_sample",
    )(h, lw['g_mix'], lw['wa'], lw['wfl'], lw['wg'], lw['bf3'], lw['conv_w'], um2, um1)


def _attention_kernel(pt_ref, qa_ref, ka_ref, va_ref, qs_ref, kn_ref, vn_ref, ln_ref, rep_ref, ut_ref,
                      *refs, tq, nq, nsplit, npg):
    del pt_ref
    k_refs, v_refs, lf_refs = refs[0:npg], refs[npg:2 * npg], refs[2 * npg:3 * npg]
    o_ref, os_ref = refs[3 * npg:]
    qi = pl.program_id(2)
    rq = tq // nsplit
    low = lax.broadcasted_iota(jnp.int32, (rq, LANES), 1) < HEAD_DIM
    units = [(r, slice(par * LANES, (par + 1) * LANES)) for r in range(nsplit) for par in range(2)]

    causal = (lax.broadcasted_iota(jnp.int32, (rq, rq), 1)
              <= lax.broadcasted_iota(jnp.int32, (rq, rq), 0))

    def scores(r, hs, past):
        nv = past + r * rq
        q = qa_ref[r * rq:(r + 1) * rq, hs]
        out = [jnp.where(causal, _dot_nt(q, ka_ref[nv:nv + rq, hs]), -jnp.inf)]
        if nv:
            out.append(_dot_nt(q, ka_ref[0:nv, hs]))
        return out

    def probs(ss):
        m = jnp.max(ss[0], axis=1, keepdims=True)
        for s in ss[1:]:
            m = jnp.maximum(m, jnp.max(s, axis=1, keepdims=True))
        return [jnp.exp2(s - m).astype(BF16) for s in ss]

    def values(ps, hs):
        nv = ps[1].shape[1] if len(ps) > 1 else 0
        acc = _dot(ps[0], va_ref[nv:nv + rq, hs])
        if nv:
            acc = acc + _dot(ps[1], va_ref[0:nv, hs])
        return acc

    for c in range(nq):
        @pl.when(qi == c)
        def _(c=c):
            sample = _sample_stages(qs_ref, kn_ref, vn_ref, ln_ref, rep_ref, ut_ref, k_refs, v_refs,
                                    lf_refs, os_ref, SAMPLE_PAGE_GROUPS)
            past = c * tq
            s, p, acc = {}, {}, {}
            slots = len(units) + 2
            for t in range(max(slots, len(sample))):
                if t < len(units):
                    s[t] = scores(*units[t], past)
                if 1 <= t <= len(units):
                    p[t - 1] = probs(s.pop(t - 1))
                if 2 <= t < slots:
                    acc[t - 2] = values(p.pop(t - 2), units[t - 2][1])
                if t < len(sample):
                    sample[t]()
            for r in range(nsplit):
                a_e, a_o = acc[2 * r], acc[2 * r + 1]
                o_e = a_e / a_e[:, HEAD_DIM:HEAD_DIM + 1]
                o_o = a_o / a_o[:, 0:1]
                o_ref[r * rq:(r + 1) * rq, :] = jnp.where(low, o_e, o_o).astype(o_ref.dtype)


def _attention(layer, page_table, qa, ka, va, qs, kn, vn, ln, ck, cv, clf, rep, ut, tq):
    b, l, d2 = qa.shape
    d = d2 // 2
    nb, nh, hd = qs.shape
    npair = nh // 2
    nq = l // tq
    npg = page_table.shape[1]
    page = ck.shape[-1]
    assert nb == b * npair * nq, "one sample sequence rides along with each prompt attention step"
    pt = page_table.reshape(-1)

    seq = lambda i, p, j: (i * npair + p) * nq + j

    def page_map(n, nd):
        return lambda i, p, j, pt_ref: (layer, pt_ref[seq(i, p, j) * npg + n]) + (0,) * nd

    per_seq = lambda shape: pl.BlockSpec((None,) + shape, lambda i, p, j, pt_ref: (seq(i, p, j), 0, 0))
    const = lambda a: pl.BlockSpec(a.shape, lambda i, p, j, pt_ref: (0,) * a.ndim)
    in_specs = ([pl.BlockSpec((None, tq, 2 * LANES), lambda i, p, j, pt_ref: (i, j, p)),
                 pl.BlockSpec((None, l, 2 * LANES), lambda i, p, j, pt_ref: (i, 0, p)),
                 pl.BlockSpec((None, l, 2 * LANES), lambda i, p, j, pt_ref: (i, 0, p)),
                 per_seq((nh, hd)), per_seq((1, d)), per_seq((1, d)), per_seq((nh, 1)),
                 const(rep), const(ut)]
                + [pl.BlockSpec((None, None, nh, hd, page), page_map(n, 3)) for n in range(npg)]
                + [pl.BlockSpec((None, None, nh, hd, page), page_map(n, 3)) for n in range(npg)]
                + [pl.BlockSpec((None, None, nh, page), page_map(n, 2)) for n in range(npg)])
    grid_spec = pltpu.PrefetchScalarGridSpec(
        num_scalar_prefetch=1, grid=(b, npair, nq), in_specs=in_specs,
        out_specs=[pl.BlockSpec((None, tq, LANES), lambda i, p, j, pt_ref: (i, j, p)), per_seq((1, d))])
    return pl.pallas_call(
        functools.partial(_attention_kernel, tq=tq, nq=nq, nsplit=ATTN_ROW_PARTS, npg=npg),
        grid_spec=grid_spec,
        out_shape=[jax.ShapeDtypeStruct((b, l, d), BF16), jax.ShapeDtypeStruct((nb, 1, d), F32)],
        compiler_params=pltpu.CompilerParams(
            dimension_semantics=("arbitrary", "arbitrary", "arbitrary"),
            vmem_limit_bytes=VMEM_LIMIT),
        name="attention",
    )(pt, qa, ka, va, qs, kn, vn, ln, rep, ut, *([ck] * npg), *([cv] * npg), *([clf] * npg))


def _sample_stages(q_ref, kn_ref, vn_ref, ln_ref, rep_ref, ut_ref, k_refs, v_refs, lf_refs, o_ref,
                   ngroups):
    npg = len(k_refs)
    nh = N_HEADS
    d = nh * HEAD_DIM
    page = k_refs[0].shape[-1]
    per = -(-npg // ngroups)
    groups = [range(g * per, min((g + 1) * per, npg)) for g in range(ngroups)]
    st = {}

    def prepare():
        st['own'] = (lax.broadcasted_iota(jnp.int32, (nh, d), 1) // HEAD_DIM
                     == lax.broadcasted_iota(jnp.int32, (nh, d), 0))
        st['qrep'] = _dot(q_ref[...].astype(BF16), rep_ref[...])
        st['qbd'] = jnp.where(st['own'], st['qrep'], 0.0).astype(BF16)
        st['scores'] = [None] * npg
        pieces = []
        for p in range(npg):
            pieces.extend(_split3(lf_refs[p][...]))
        st['sfx'] = _dot(jnp.concatenate(pieces, axis=0).astype(BF16), ut_ref[...])

    def score_pages(pages):
        for p in pages:
            st['scores'][p] = _dot(st['qbd'], k_refs[p][...].reshape(d, page).astype(BF16))

    def softmax():
        own, qrep, sfx_all = st['own'], st['qrep'], st['sfx']
        car = ln_ref[...]
        z = [None] * npg
        for p in reversed(range(npg)):
            blk = sfx_all[p * N_PIECES * nh:(p + 1) * N_PIECES * nh]
            sfx = blk[0:nh] + blk[nh:2 * nh] + blk[2 * nh:3 * nh]
            z[p] = st['scores'][p] + sfx[:, 0:page] + car
            car = car + sfx[:, page:page + 1]
        s_new = jnp.sum(jnp.where(own, qrep * kn_ref[...], 0.0), axis=1, keepdims=True)
        m = s_new
        for p in range(npg):
            m = jnp.maximum(m, jnp.max(z[p], axis=1, keepdims=True))
        e_new = jnp.exp(s_new - m)
        st['e'] = [jnp.exp(z[p] - m) for p in range(npg)]
        l = e_new
        for p in range(npg):
            l = l + jnp.sum(st['e'][p], axis=1, keepdims=True)
        st['l'] = l
        st['acc'] = e_new * vn_ref[...]

    def value_pages(pages):
        for p in pages:
            st['acc'] = st['acc'] + _dot_nt(st['e'][p].astype(BF16),
                                            v_refs[p][...].reshape(d, page).astype(BF16))

    def finish():
        o_ref[...] = jnp.sum(jnp.where(st['own'], st['acc'] / st['l'], 0.0), axis=0, keepdims=True)

    stages = [prepare]
    stages += [functools.partial(score_pages, g) for g in groups]
    stages += [softmax]
    stages += [functools.partial(value_pages, g) for g in groups]
    stages += [finish]
    return stages


def _sample_constants(page):
    rep = np.tile(np.eye(HEAD_DIM, dtype=np.float32), (1, N_HEADS))
    ut = np.ones((page, 2 * page), np.float32)
    ut[:, :page] = np.arange(page)[:, None] > np.arange(page)[None, :]
    return jnp.asarray(rep, BF16), jnp.asarray(ut, BF16)


def _outproj_kernel(h_ref, yc_ref, o_ref, sgc_ref, sga_ref, p_ref, wco_ref, wao_ref, wo_ref,
                    gffn_ref, wgt_ref, wup_ref, wdn_ref, gple_ref, wpg_ref, wpp_ref, gfin_ref,
                    out_ref, *, final):
    h = h_ref[...]
    merged = (sgc_ref[...].astype(F32) * _dot(yc_ref[...], wco_ref[...])
              + sga_ref[...].astype(F32) * _dot(o_ref[...].astype(BF16), wao_ref[...]))
    h = h + _dot(merged.astype(BF16), wo_ref[...])
    hn = _rms(h, gffn_ref[...]).astype(BF16)
    gate = _dot(hn, wgt_ref[...])
    act = (gate * _sigmoid(gate) * _dot(hn, wup_ref[...])).astype(BF16)
    h = h + _dot(act, wdn_ref[...])
    hn = _rms(h, gple_ref[...]).astype(BF16)
    h = h + _sigmoid(_dot(hn, wpg_ref[...])) * _dot(p_ref[...].astype(BF16), wpp_ref[...])
    out_ref[...] = _rms(h, gfin_ref[...]) if final else h


def _outproj(h, yc, o, sgc, sga, p, lw, g_final, tm, final):
    m, d = h.shape
    dple = p.shape[1]
    row = lambda n: pl.BlockSpec((tm, n), lambda i: (i, 0))
    ws = [lw['wco'], lw['wao'], lw['wo'], lw['g_ffn'], lw['wgt'], lw['wup'], lw['wdn'],
          lw['g_ple'], lw['wpg'], lw['wpp'], g_final]
    return pl.pallas_call(
        functools.partial(_outproj_kernel, final=final),
        grid=(m // tm,),
        in_specs=[row(d), row(d), row(d), row(d), row(d), row(dple)] + [_resident(w.shape) for w in ws],
        out_specs=row(d),
        out_shape=jax.ShapeDtypeStruct((m, d), F32),
        compiler_params=pltpu.CompilerParams(dimension_semantics=("arbitrary",),
                                             vmem_limit_bytes=VMEM_LIMIT),
        name="outproj",
    )(h, yc, o, sgc, sga, p, *ws)


TM_IN = 256
TQ = 1024
ATTN_ROW_PARTS = 4
SAMPLE_PAGE_GROUPS = 3
TM_OUT = 256


def _layer_weights(i, w_in, b_forget, conv_w, w_conv_out, w_attn_out, w_o, g_mix, g_ffn, w_gate,
                   w_up, w_down, g_ple, w_ple_gate, w_ple_proj):
    d = w_in.shape[1]
    nh = N_HEADS
    wi = w_in[i]
    main = 6 * d
    return dict(
        wa=wi[:, :main].astype(BF16),
        wfl=jnp.tile(wi[:, main:main + nh], (1, N_PIECES)).astype(BF16),
        wg=wi[:, main + nh:].astype(BF16),
        bf3=jnp.tile(b_forget[i][None, :], (1, N_PIECES)),
        conv_w=conv_w[i], g_mix=g_mix[i][None, :], g_ffn=g_ffn[i][None, :], g_ple=g_ple[i][None, :],
        wco=w_conv_out[i].astype(BF16), wao=w_attn_out[i].astype(BF16), wo=w_o[i].astype(BF16),
        wgt=w_gate[i].astype(BF16), wup=w_up[i].astype(BF16), wdn=w_down[i].astype(BF16),
        wpg=w_ple_gate[i].astype(BF16), wpp=w_ple_proj[i].astype(BF16))


def kernel(x_prompt, x_sample, cache_k, cache_v, cache_logf, state_conv, page_table, p_prompt, p_sample, w_in, b_forget, conv_w, w_conv_out, w_attn_out, w_o, g_mix, g_ffn, w_gate, w_up, w_down, g_ple, w_ple_gate, w_ple_proj, g_final):
    depth = w_in.shape[0]
    b, l, d = x_prompt.shape
    nb = x_sample.shape[0]
    nh, hd = N_HEADS, HEAD_DIM
    tm_in, tq, tm_out = min(TM_IN, l), min(TQ, l), min(TM_OUT, l)
    assert x_sample.shape[1] == 1 and d == nh * hd
    assert l % tm_in == 0 and l % tq == 0 and (b * l) % tm_out == 0

    tri = jnp.tril(jnp.ones((tm_in, tm_in), F32)).astype(BF16)
    consts = _extras_constants(d) + (tri,)
    rep, ut = _sample_constants(cache_k.shape[2])
    gfin = g_final[None, :]
    ck = jnp.transpose(cache_k, (0, 1, 3, 4, 2))
    cv = jnp.transpose(cache_v, (0, 1, 3, 4, 2))
    clf = jnp.transpose(cache_logf, (0, 1, 3, 2))

    hp = x_prompt
    hs = x_sample.reshape(nb, d)
    outs = [[] for _ in range(6)]
    kt_all = jnp.zeros((depth, b, d, l), F32)
    vt_all = jnp.zeros((depth, b, d, l), F32)
    for i in range(depth):
        lw = _layer_weights(i, w_in, b_forget, conv_w, w_conv_out, w_attn_out, w_o, g_mix, g_ffn,
                            w_gate, w_up, w_down, g_ple, w_ple_gate, w_ple_proj)
        final = i == depth - 1

        kt_all, vt_all, lf, qa, ka, va, yc, sgc, sga, cl = _inproj_prompt(
            hp, lw, consts, tm_in, i, depth, kt_all, vt_all)
        ks, vs, lfs, qs, us, ycs, sgcs, sgas = _inproj_sample(
            hs, state_conv[i, :, 0, :], state_conv[i, :, 1, :], lw)
        o, os_ = _attention(i, page_table, qa, ka, va, qs.reshape(nb, nh, hd), ks.reshape(nb, 1, d),
                            vs.reshape(nb, 1, d), lfs.reshape(nb, nh, 1), ck, cv, clf, rep, ut, tq)
        flat = lambda a: a.reshape(b * l, a.shape[-1])
        hp = _outproj(flat(hp), flat(yc), flat(o), flat(sgc), flat(sga), flat(p_prompt[i]), lw, gfin,
                      tm_out, final).reshape(b, l, d)
        hs = _outproj(hs, ycs, os_.reshape(nb, d), sgcs, sgas, p_sample[i].reshape(nb, -1), lw, gfin,
                      nb, final)

        outs[0].append(lf)
        outs[1].append(cl)
        outs[2].append(ks.reshape(nb, 1, nh, hd))
        outs[3].append(vs.reshape(nb, 1, nh, hd))
        outs[4].append(lfs.reshape(nb, 1, nh))
        outs[5].append(jnp.stack([state_conv[i, :, 1, :], us], axis=1))
    per_token = lambda t: jnp.transpose(t.reshape(depth, b, nh, hd, l), (0, 1, 4, 2, 3))
    lf_p, cl_p, k_s, v_s, lf_s, cl_s = (jnp.stack(o_) for o_ in outs)
    return (hp, hs.reshape(nb, 1, d), per_token(kt_all), per_token(vt_all), lf_p, cl_p, k_s, v_s, lf_s, cl_s)
```
